```python
import math
import jax, jax.numpy as jnp
from jax import lax
import numpy as np

D_MODEL = 1024
BATCH = 2
SEQ = 8192
DEPTH = 2
DEC_BATCH = 32
DEC_SEQ = 1
PAST_LEN = 16384
PAGE_SIZE = 128

D_HEAD = 64
H_A = 8
C_A = H_A * D_HEAD
D_DECAY_LORA = 64
D_AAA_LORA = 64
D_GATE_LORA = 128
C_RWKV = 3 * C_A + D_DECAY_LORA + D_AAA_LORA + D_GATE_LORA
RWKV_SPLITS = (C_A, 2 * C_A, 3 * C_A, 3 * C_A + D_DECAY_LORA, 3 * C_A + D_DECAY_LORA + D_AAA_LORA)
H_B = 4
C_B = H_B * 2 * D_HEAD
C_DIFF = 3 * C_B
H_C = 6
C_C = H_C * D_HEAD
C_SB = 3 * C_C
N_BRANCH = 3
C_GATE = N_BRANCH * D_MODEL
C_IN = C_RWKV + C_DIFF + C_SB + C_GATE
IN_SPLITS = (C_RWKV, C_RWKV + C_DIFF, C_RWKV + C_DIFF + C_SB)
Q_BLOCK = 128
N_EXPERTS = 32
TOP_K = 4
D_FF = D_MODEL
SWIGLU_LIMIT = 7.0
SWIGLU_ALPHA = 1.702
MOE_BLOCK = 512
P_DIM = 256
RMS_EPS = 1e-6
GN_EPS = 64e-5

kernel_name = 'hybrid_rwkv7_diffattn_stickbreak_moe_step'


def _rmsnorm(x, g):
    xf = x.astype(jnp.float32)
    y = xf * lax.rsqrt(jnp.mean(xf * xf, axis=-1, keepdims=True) + RMS_EPS)
    return (y * g.astype(jnp.float32)).astype(x.dtype)


def _token_shift(z, prev, mu):
    z_prev = jnp.concatenate([prev[:, None, :].astype(z.dtype), z[:, :-1]], axis=1)
    return z + (z_prev - z) * mu


def _wkv7_step(S, inp):
    r, w, k, v, kk, a = inp
    sa = jnp.einsum('bhij,bhj->bhi', S, -kk)
    S = S * w[:, :, None, :] + sa[..., None] * (kk * a)[:, :, None, :] + v[..., None] * k[:, :, None, :]
    y = jnp.einsum('bhij,bhj->bhi', S, r)
    return S, y


def _rwkv7_mixer(z, shift0, wkv0, mu, w0, w2, a0, a2, g2, k_k, k_a, r_k, lnx_g, lnx_b):
    bsz, t_len, _ = z.shape
    zs = _token_shift(z, shift0, mu)
    r, k, v, w_lo, a_lo, g_lo = jnp.split(zs, RWKV_SPLITS, axis=-1)
    w_log = -jax.nn.softplus(-(w0 + jnp.tanh(w_lo) @ w2)) - 0.5
    decay = jnp.exp(-jnp.exp(w_log.astype(jnp.float32)))
    a = jax.nn.sigmoid(a0 + a_lo @ a2)
    g = jax.nn.sigmoid(g_lo) @ g2
    heads = lambda t: t.astype(jnp.float32).reshape(bsz, t_len, H_A, D_HEAD)
    kk = heads(k * k_k)
    kk = kk / jnp.maximum(jnp.sqrt(jnp.sum(kk * kk, axis=-1, keepdims=True)), 1e-12)
    k = k * (1 + (a - 1) * k_a)
    r_h, w_h, k_h, v_h, a_h = heads(r), heads(decay), heads(k), heads(v), heads(a)
    swap = lambda t: jnp.moveaxis(t, 1, 0)
    wkv_T, y = lax.scan(_wkv7_step, wkv0.astype(jnp.float32),
                        (swap(r_h), swap(w_h), swap(k_h), swap(v_h), swap(kk), swap(a_h)))
    y = swap(y)
    mean = jnp.mean(y, axis=-1, keepdims=True)
    var = jnp.mean(jnp.square(y - mean), axis=-1, keepdims=True)
    yn = ((y - mean) * lax.rsqrt(var + GN_EPS)).reshape(bsz, t_len, C_A) * lnx_g + lnx_b
    bonus = jnp.sum(r_h * k_h * r_k, axis=-1, keepdims=True) * v_h
    out = (yn + bonus.reshape(bsz, t_len, C_A)) * g
    return out.astype(z.dtype), z[:, -1], wkv_T.astype(wkv0.dtype)


def _sweep_queries(fn, q, q_pos):
    t_q = q.shape[1]
    if t_q <= Q_BLOCK or t_q % Q_BLOCK:
        return fn(q, q_pos)
    nb = t_q // Q_BLOCK
    qb = jnp.moveaxis(q.reshape((q.shape[0], nb, Q_BLOCK) + q.shape[2:]), 1, 0)
    pb = q_pos.reshape(nb, Q_BLOCK)
    ob = lax.map(lambda args: fn(args[0], args[1]), (qb, pb))
    ob = jnp.moveaxis(ob, 0, 1)
    return ob.reshape((q.shape[0], t_q) + ob.shape[3:])


def _split_keys(att, segs):
    bounds = np.cumsum([s[2].shape[0] for s in segs])[:-1]
    return jnp.split(att, bounds, axis=-1)


def _diff_attention(q, q_pos, segs, lam):
    s = jnp.concatenate([jnp.einsum('bqhmd,bkhmd->bhmqk', q, kk).astype(jnp.float32) for kk, _, _ in segs], axis=-1)
    s = s * (D_HEAD ** -0.5)
    mask = jnp.concatenate([kp[None, :] <= q_pos[:, None] for _, _, kp in segs], axis=-1)
    p = jax.nn.softmax(jnp.where(mask, s, -jnp.inf), axis=-1)
    att = p[:, :, 0] - lam * p[:, :, 1]
    parts = _split_keys(att, segs)
    return sum(jnp.einsum('bhqk,bkhe->bqhe', a_s.astype(vv.dtype), vv) for a_s, (_, vv, _) in zip(parts, segs))


def _stick_breaking(q, q_pos, segs):
    z = jnp.concatenate([jnp.einsum('bqhd,bkhd->bhqk', q, kk).astype(jnp.float32) for kk, _, _ in segs], axis=-1)
    z = z * (D_HEAD ** -0.5)
    mask = jnp.concatenate([kp[None, :] < q_pos[:, None] for _, _, kp in segs], axis=-1)
    log_1mb = jnp.where(mask, jax.nn.log_sigmoid(-z), 0.0)
    suffix = lax.cumsum(log_1mb, axis=3, reverse=True) - log_1mb
    att = jnp.where(mask, jnp.exp(jax.nn.log_sigmoid(z) + suffix), 0.0)
    parts = _split_keys(att, segs)
    return sum(jnp.einsum('bhqk,bkhd->bqhd', a_s.astype(vv.dtype), vv) for a_s, (_, vv, _) in zip(parts, segs))


def _moe_ffn(x, router_w, router_b, w_gu, b_gu, w_down, b_down):
    bsz, t_len, d = x.shape
    n_tok = bsz * t_len
    xt = x.reshape(n_tok, d)
    logits = (xt @ router_w + router_b).astype(jnp.float32)
    top_val, top_idx = lax.top_k(logits, TOP_K)
    gate = jax.nn.softmax(top_val, axis=-1)
    m = n_tok * TOP_K
    blk = max(8, min(MOE_BLOCK, m // N_EXPERTS))
    n_blk = -(-m // blk) + N_EXPERTS
    flat_e = top_idx.reshape(m)
    flat_g = gate.reshape(m)
    order = jnp.argsort(flat_e)
    e_sorted = flat_e[order]
    counts = jnp.bincount(flat_e, length=N_EXPERTS)
    padded = (counts + blk - 1) // blk * blk
    pad_end = jnp.cumsum(padded)
    pad_start = pad_end - padded
    grp_start = jnp.cumsum(counts) - counts
    dest = pad_start[e_sorted] + jnp.arange(m) - grp_start[e_sorted]
    n_rows = n_blk * blk
    src_tok = jnp.zeros((n_rows,), jnp.int32).at[dest].set((order // TOP_K).astype(jnp.int32))
    row_w = jnp.zeros((n_rows,), jnp.float32).at[dest].set(flat_g[order])
    blk_e = jnp.minimum(jnp.searchsorted(pad_end, jnp.arange(n_blk) * blk, side='right'), N_EXPERTS - 1)
    xb = xt[src_tok].reshape(n_blk, blk, d)

    def expert_block(args):
        xblk, e = args
        hh = xblk @ w_gu[e] + b_gu[e]
        hg = jnp.minimum(hh[..., :D_FF], SWIGLU_LIMIT)
        hl = jnp.clip(hh[..., D_FF:], -SWIGLU_LIMIT, SWIGLU_LIMIT)
        act = hg * jax.nn.sigmoid(SWIGLU_ALPHA * hg) * (hl + 1)
        return act @ w_down[e] + b_down[e]

    yb = lax.map(expert_block, (xb, blk_e)).reshape(n_rows, d)
    y = jax.ops.segment_sum(yb * row_w[:, None].astype(yb.dtype), src_tok, num_segments=n_tok)
    return y.reshape(bsz, t_len, d).astype(x.dtype)


def _layer(h, pe, q_pos, past_b, past_c, shift0, wkv0, lam_init, W):
    bsz, t_len, _ = h.shape
    xn = _rmsnorm(h, W['norm_mix_g'])
    z = xn @ W['w_in']
    z_a, z_b, z_c, z_g = jnp.split(z, IN_SPLITS, axis=-1)
    o_a, shift_new, wkv_new = _rwkv7_mixer(z_a, shift0, wkv0, W['rwkv_mu'], W['rwkv_w0'], W['rwkv_w2'],
                                           W['rwkv_a0'], W['rwkv_a2'], W['rwkv_g2'], W['rwkv_k_k'],
                                           W['rwkv_k_a'], W['rwkv_r_k'], W['rwkv_lnx_g'], W['rwkv_lnx_b'])
    q_b, k_b, v_b = jnp.split(z_b, 3, axis=-1)
    q_b = _rmsnorm(q_b.reshape(bsz, t_len, H_B, 2, D_HEAD), W['diff_q_norm'])
    k_b = _rmsnorm(k_b.reshape(bsz, t_len, H_B, 2, D_HEAD), W['diff_k_norm'])
    v_b = v_b.reshape(bsz, t_len, H_B, 2 * D_HEAD)
    lv = W['diff_lambda'].astype(jnp.float32)
    lam = jnp.exp(jnp.sum(lv[0] * lv[1])) - jnp.exp(jnp.sum(lv[2] * lv[3])) + lam_init
    segs_b = past_b + ((k_b, v_b, q_pos),)
    o_b = _sweep_queries(lambda qq, pp: _diff_attention(qq, pp, segs_b, lam), q_b, q_pos)
    o_b = (_rmsnorm(o_b, W['diff_subln_g']) * (1.0 - lam_init)).reshape(bsz, t_len, C_B)
    q_c, k_c, v_c = [t.reshape(bsz, t_len, H_C, D_HEAD) for t in jnp.split(z_c, 3, axis=-1)]
    segs_c = past_c + ((k_c, v_c, q_pos),)
    o_c = _sweep_queries(lambda qq, pp: _stick_breaking(qq, pp, segs_c), q_c, q_pos).reshape(bsz, t_len, C_C)
    gates = jax.nn.sigmoid(z_g.reshape(bsz, t_len, N_BRANCH, D_MODEL))
    mix = (gates[:, :, 0] * (o_a @ W['w_branch_a']) + gates[:, :, 1] * (o_b @ W['w_branch_b'])
           + gates[:, :, 2] * (o_c @ W['w_branch_c']))
    h = h + mix @ W['w_out']
    h = h + _moe_ffn(_rmsnorm(h, W['norm_ffn_g']), W['router_w'], W['router_b'], W['moe_w_gu'],
                     W['moe_b_gu'], W['moe_w_down'], W['moe_b_down'])
    h = h + (pe @ W['ple_w']) * jax.nn.sigmoid(_rmsnorm(h, W['norm_ple_g']) @ W['ple_gate_w'])
    new_rows = (k_b.reshape(bsz, t_len, H_B, 2 * D_HEAD), v_b, k_c, v_c, wkv_new, shift_new)
    return h, new_rows


def setup_inputs(seed: int = 0) -> dict:
    key = jax.random.key(seed)
    ks = iter(jax.random.split(key, 64))

    def nrm(shape, scale=1.0):
        return jax.random.normal(next(ks), shape, jnp.float32) * scale

    def gain(shape):
        return 1.0 + nrm(shape, 0.05)

    n_pages = PAST_LEN // PAGE_SIZE
    n_pool = (5 * DEC_BATCH * n_pages) // 4
    L = DEPTH
    inp = {}
    inp['x_prompt'] = nrm((BATCH, SEQ, D_MODEL))
    inp['x_sample'] = nrm((DEC_BATCH, DEC_SEQ, D_MODEL))
    inp['cache_diff_k'] = nrm((n_pool, L, PAGE_SIZE, H_B, 2 * D_HEAD))
    inp['cache_diff_v'] = nrm((n_pool, L, PAGE_SIZE, H_B, 2 * D_HEAD))
    inp['cache_sb_k'] = nrm((n_pool, L, PAGE_SIZE, H_C, D_HEAD))
    inp['cache_sb_v'] = nrm((n_pool, L, PAGE_SIZE, H_C, D_HEAD))
    inp['state_wkv'] = nrm((DEC_BATCH, L, H_A, D_HEAD, D_HEAD), 0.3)
    inp['state_shift'] = nrm((DEC_BATCH, L, C_RWKV))
    perm = jax.random.permutation(next(ks), n_pool)[: DEC_BATCH * n_pages]
    inp['page_table'] = perm.reshape(DEC_BATCH, n_pages).astype(jnp.int32)
    inp['p_prompt'] = nrm((L, BATCH, SEQ, P_DIM))
    inp['p_sample'] = nrm((L, DEC_BATCH, DEC_SEQ, P_DIM))
    inp['norm_mix_g'] = gain((L, D_MODEL))
    inp['w_in'] = nrm((L, D_MODEL, C_IN), D_MODEL ** -0.5)
    inp['rwkv_mu'] = jax.random.uniform(next(ks), (L, C_RWKV), jnp.float32)
    inp['rwkv_w0'] = nrm((L, C_A), 0.5)
    inp['rwkv_w2'] = nrm((L, D_DECAY_LORA, C_A), D_DECAY_LORA ** -0.5)
    inp['rwkv_a0'] = nrm((L, C_A), 0.5)
    inp['rwkv_a2'] = nrm((L, D_AAA_LORA, C_A), D_AAA_LORA ** -0.5)
    inp['rwkv_g2'] = nrm((L, D_GATE_LORA, C_A), D_GATE_LORA ** -0.5)
    inp['rwkv_k_k'] = 0.85 + nrm((L, C_A), 0.05)
    inp['rwkv_k_a'] = gain((L, C_A))
    inp['rwkv_r_k'] = nrm((L, H_A, D_HEAD), 0.1)
    inp['rwkv_lnx_g'] = gain((L, C_A))
    inp['rwkv_lnx_b'] = nrm((L, C_A), 0.02)
    inp['diff_q_norm'] = gain((L, D_HEAD))
    inp['diff_k_norm'] = gain((L, D_HEAD))
    inp['diff_lambda'] = nrm((L, 4, D_HEAD), 0.1)
    inp['diff_subln_g'] = gain((L, 2 * D_HEAD))
    inp['w_branch_a'] = nrm((L, C_A, D_MODEL), C_A ** -0.5)
    inp['w_branch_b'] = nrm((L, C_B, D_MODEL), C_B ** -0.5)
    inp['w_branch_c'] = nrm((L, C_C, D_MODEL), C_C ** -0.5)
    inp['w_out'] = nrm((L, D_MODEL, D_MODEL), 0.5 * D_MODEL ** -0.5)
    inp['norm_ffn_g'] = gain((L, D_MODEL))
    inp['router_w'] = nrm((L, D_MODEL, N_EXPERTS), D_MODEL ** -0.5)
    inp['router_b'] = nrm((L, N_EXPERTS), 0.01)
    inp['moe_w_gu'] = nrm((L, N_EXPERTS, D_MODEL, 2 * D_FF), D_MODEL ** -0.5)
    inp['moe_b_gu'] = nrm((L, N_EXPERTS, 2 * D_FF), 0.01)
    inp['moe_w_down'] = nrm((L, N_EXPERTS, D_FF, D_MODEL), 0.5 * D_FF ** -0.5)
    inp['moe_b_down'] = nrm((L, N_EXPERTS, D_MODEL), 0.01)
    inp['norm_ple_g'] = gain((L, D_MODEL))
    inp['ple_w'] = nrm((L, P_DIM, D_MODEL), P_DIM ** -0.5)
    inp['ple_gate_w'] = nrm((L, D_MODEL, D_MODEL), D_MODEL ** -0.5)
    return inp


def reference(x_prompt, x_sample, cache_diff_k, cache_diff_v, cache_sb_k, cache_sb_v, state_wkv, state_shift,
              page_table, p_prompt, p_sample, norm_mix_g, w_in, rwkv_mu, rwkv_w0, rwkv_w2, rwkv_a0, rwkv_a2,
              rwkv_g2, rwkv_k_k, rwkv_k_a, rwkv_r_k, rwkv_lnx_g, rwkv_lnx_b, diff_q_norm, diff_k_norm,
              diff_lambda, diff_subln_g, w_branch_a, w_branch_b, w_branch_c, w_out, norm_ffn_g, router_w,
              router_b, moe_w_gu, moe_b_gu, moe_w_down, moe_b_down, norm_ple_g, ple_w, ple_gate_w):
    bsz_p, seq_p, _ = x_prompt.shape
    bsz_s, seq_s, _ = x_sample.shape
    past_len = page_table.shape[1] * PAGE_SIZE
    pos_p = jnp.arange(seq_p, dtype=jnp.int32)
    pos_s = PAST_LEN + jnp.arange(seq_s, dtype=jnp.int32)
    pos_past = jnp.arange(past_len, dtype=jnp.int32)
    shift_zero = jnp.zeros((bsz_p, C_RWKV), x_prompt.dtype)
    wkv_zero = jnp.zeros((bsz_p, H_A, D_HEAD, D_HEAD), x_prompt.dtype)

    def gather(cache, i):
        g = cache[page_table, i]
        return g.reshape((bsz_s, past_len) + g.shape[3:])

    h_p, h_s = x_prompt, x_sample
    rows_p, rows_s = [], []
    for i in range(DEPTH):
        W = dict(norm_mix_g=norm_mix_g[i], w_in=w_in[i], rwkv_mu=rwkv_mu[i], rwkv_w0=rwkv_w0[i],
                 rwkv_w2=rwkv_w2[i], rwkv_a0=rwkv_a0[i], rwkv_a2=rwkv_a2[i], rwkv_g2=rwkv_g2[i],
                 rwkv_k_k=rwkv_k_k[i], rwkv_k_a=rwkv_k_a[i], rwkv_r_k=rwkv_r_k[i], rwkv_lnx_g=rwkv_lnx_g[i],
                 rwkv_lnx_b=rwkv_lnx_b[i], diff_q_norm=diff_q_norm[i], diff_k_norm=diff_k_norm[i],
                 diff_lambda=diff_lambda[i], diff_subln_g=diff_subln_g[i], w_branch_a=w_branch_a[i],
                 w_branch_b=w_branch_b[i], w_branch_c=w_branch_c[i], w_out=w_out[i], norm_ffn_g=norm_ffn_g[i],
                 router_w=router_w[i], router_b=router_b[i], moe_w_gu=moe_w_gu[i], moe_b_gu=moe_b_gu[i],
                 moe_w_down=moe_w_down[i], moe_b_down=moe_b_down[i], norm_ple_g=norm_ple_g[i],
                 ple_w=ple_w[i], ple_gate_w=ple_gate_w[i])
        lam_init = 0.8 - 0.6 * math.exp(-0.3 * i)
        h_p, r_p = _layer(h_p, p_prompt[i], pos_p, (), (), shift_zero, wkv_zero, lam_init, W)
        kd = gather(cache_diff_k, i).reshape(bsz_s, past_len, H_B, 2, D_HEAD)
        vd = gather(cache_diff_v, i)
        kc = gather(cache_sb_k, i)
        vc = gather(cache_sb_v, i)
        h_s, r_s = _layer(h_s, p_sample[i], pos_s, ((kd, vd, pos_past),), ((kc, vc, pos_past),),
                          state_shift[:, i], state_wkv[:, i], lam_init, W)
        rows_p.append(r_p)
        rows_s.append(r_s)

    st = lambda rows, j: jnp.stack([r[j] for r in rows], axis=1)
    return (h_p, h_s,
            st(rows_p, 0), st(rows_p, 1), st(rows_p, 2), st(rows_p, 3), st(rows_p, 4), st(rows_p, 5),
            st(rows_s, 0), st(rows_s, 1), st(rows_s, 2), st(rows_s, 3), st(rows_s, 4), st(rows_s, 5))
```

```python
import functools
import math

import numpy as np
import jax
import jax.numpy as jnp
from jax import lax
from jax.experimental import pallas as pl
from jax.experimental.pallas import tpu as pltpu

F32 = jnp.float32
BF16 = jnp.bfloat16
I32 = jnp.int32

D_HEAD = 64
H_A, H_B, H_C = 8, 4, 6
C_A, C_B, C_C = H_A * D_HEAD, H_B * 2 * D_HEAD, H_C * D_HEAD
D_DECAY_LORA, D_AAA_LORA, D_GATE_LORA = 64, 64, 128
C_RWKV = 3 * C_A + D_DECAY_LORA + D_AAA_LORA + D_GATE_LORA
N_EXPERTS, TOP_K = 32, 4
SWIGLU_LIMIT, SWIGLU_ALPHA = 7.0, 1.702
PAGE_SIZE = 128
RMS_EPS = 1e-6
GN_EPS = 64e-5

LANES = 128
SUBLANES = 8
ROW_TILES = 8

ZA0, ZA_W = 0, 2048
QB0, KB0, VB0 = 2048, 2560, 3072
QC0, KC0, VC0 = 3584, 4096, 4608
ZG0 = 5120
ZW = 8192

SB_DEAD = -110.0
NEG_BIG = -1e30

NN = (((1,), (0,)), ((), ()))
NT = (((1,), (1,)), ((), ()))
TN = (((0,), (0,)), ((), ()))

VMEM_LIMIT = 56 * 1024 * 1024


def _cp(sem, vmem=None):
    return pltpu.CompilerParams(dimension_semantics=sem, vmem_limit_bytes=vmem)


def _dot(a, b, dims=NN):
    return lax.dot_general(a, b, dims, preferred_element_type=F32)


def _split(x):
    hi = x.astype(BF16)
    lo = (x - hi.astype(F32)).astype(BF16)
    return hi, lo


def _mm(a, b, dims=NN, passes=1):
    if passes == 1:
        return _dot(a.astype(BF16), b.astype(BF16), dims)
    ah, al = _split(a)
    bh, bl = _split(b)
    return _dot(ah, bh, dims) + (_dot(ah, bl, dims) + _dot(al, bh, dims))


def _mm_exact_rhs(a, b_exact, passes=2):
    hi = a.astype(BF16)
    out = _dot(hi, b_exact)
    rem = a - hi.astype(F32)
    for _ in range(passes - 1):
        part = rem.astype(BF16)
        out = out + _dot(part, b_exact)
        rem = rem - part.astype(F32)
    return out


def _mm_exact_lhs(a_exact, b, passes=2):
    hi = b.astype(BF16)
    out = _dot(a_exact, hi)
    rem = b - hi.astype(F32)
    for _ in range(passes - 1):
        part = rem.astype(BF16)
        out = out + _dot(a_exact, part)
        rem = rem - part.astype(F32)
    return out


def _sigmoid(x):
    return 1.0 / (1.0 + jnp.exp(-x))


def _log_sigmoid_pair(z):
    lg = jnp.log(1.0 + jnp.exp(-jnp.abs(z)))
    return jnp.minimum(z, 0.0) - lg, -jnp.maximum(z, 0.0) - lg


def _block_indicator(n, blk):
    idx = np.arange(n) // blk
    return jnp.asarray((idx[:, None] == idx[None, :]).astype(np.float32), dtype=BF16)


def _in_proj_kernel(x_ref, g_ref, w_ref, o_ref, xn_ref):
    @pl.when(pl.program_id(1) == 0)
    def _():
        x = x_ref[...]
        ms = jnp.mean(x * x, axis=-1, keepdims=True)
        xn_ref[...] = (x * lax.rsqrt(ms + RMS_EPS) * g_ref[...]).astype(BF16)

    o_ref[...] = _dot(xn_ref[...], w_ref[...])


def _in_proj(x, g, wp):
    n, d = x.shape
    c = wp.shape[1]
    tm = min(1024, n)
    tn = 512
    return pl.pallas_call(
        _in_proj_kernel,
        grid=(n // tm, c // tn),
        in_specs=[pl.BlockSpec((tm, d), lambda i, j: (i, 0)),
                  pl.BlockSpec((1, d), lambda i, j: (0, 0)),
                  pl.BlockSpec((d, tn), lambda i, j: (0, j))],
        out_specs=pl.BlockSpec((tm, tn), lambda i, j: (i, j)),
        out_shape=jax.ShapeDtypeStruct((n, c), F32),
        scratch_shapes=[pltpu.VMEM((tm, d), BF16)],
        compiler_params=_cp(("parallel", "arbitrary")),
        name="in_proj",
    )(x, g.reshape(1, d), wp)


def _pack_w_in(w):
    c_diff, c_sb = 3 * C_B, 3 * C_C
    za = w[:, :C_RWKV]
    zb = w[:, C_RWKV:C_RWKV + c_diff]
    zc = w[:, C_RWKV + c_diff:C_RWKV + c_diff + c_sb]
    zg = w[:, C_RWKV + c_diff + c_sb:]
    pad = lambda a, n: jnp.pad(a, ((0, 0), (0, n - a.shape[1])))
    parts = [pad(za, ZA_W), zb,
             pad(zc[:, :C_C], 512), pad(zc[:, C_C:2 * C_C], 512), pad(zc[:, 2 * C_C:], 512), zg]
    return jnp.concatenate(parts, axis=1).astype(BF16)


def _tri_inv(a, n, eye, row, col, passes):
    bs = min(16, n)
    if n > bs:
        same = (row // bs) == (col // bs)
        ad = jnp.where(same, a, 0.0)
        ao = a - ad
    else:
        ad, ao = a, None
    td = eye + ad
    pw = ad
    p = 2
    while p < bs:
        pw = _mm(pw, pw, NN, passes)
        td = td + _mm(td, pw, NN, passes)
        p *= 2
    if ao is None:
        return td
    nmat = _mm(td, ao, NN, passes)
    res = eye + nmat
    npw = nmat
    p = 2
    while p < n // bs:
        npw = _mm(npw, npw, NN, passes)
        res = res + _mm(res, npw, NN, passes)
        p *= 2
    return _mm(res, td, NN, passes)


def _rwkv_kernel(z_ref, shift0_ref, wkv0_ref, mu_ref, vec_ref, w2_ref, a2_ref, g2_ref, bd_ref,
                 o_ref, wkv_out_ref, shift_out_ref, carry_ref, state_ref, *, L, nc, t_real):
    c = pl.program_id(1)

    @pl.when(c == 0)
    def _():
        carry_ref[...] = shift0_ref[0]
        state_ref[...] = wkv0_ref[0]

    z = z_ref[...]
    row1 = lax.broadcasted_iota(I32, (L, 1), 0)
    zp = jnp.where(row1 == 0, carry_ref[...], pltpu.roll(z, 1, axis=0))
    carry_ref[...] = z[L - 1:L, :]
    zs = z + (zp - z) * mu_ref[...]

    r = zs[:, 0:C_A]
    k = zs[:, C_A:2 * C_A]
    v = zs[:, 2 * C_A:3 * C_A]
    o1 = 3 * C_A
    w_lo = zs[:, o1:o1 + D_DECAY_LORA]
    a_lo = zs[:, o1 + D_DECAY_LORA:o1 + D_DECAY_LORA + D_AAA_LORA]
    g_lo = zs[:, o1 + D_DECAY_LORA + D_AAA_LORA:C_RWKV]

    w0, a0 = vec_ref[0:1, :], vec_ref[1:2, :]
    k_k, k_a, r_k = vec_ref[2:3, :], vec_ref[3:4, :], vec_ref[4:5, :]
    lnx_g, lnx_b = vec_ref[5:6, :], vec_ref[6:7, :]
    bd = bd_ref[...]

    xw = w0 + _mm(jnp.tanh(w_lo), w2_ref[...])
    w_log = -(jnp.maximum(-xw, 0.0) + jnp.log(1.0 + jnp.exp(-jnp.abs(xw)))) - 0.5
    logw = -jnp.exp(w_log)
    a = _sigmoid(a0 + _mm(a_lo, a2_ref[...]))
    g = _mm(_sigmoid(g_lo), g2_ref[...])
    kk = k * k_k
    ss = _mm_exact_rhs(kk * kk, bd, 2)
    kk = kk / jnp.maximum(jnp.sqrt(ss), 1e-12)
    k2 = k * (1.0 + (a - 1.0) * k_a)
    if nc * L != t_real:
        valid = (c * L + row1) < t_real
        logw = jnp.where(valid, logw, 0.0)
        kk = jnp.where(valid, kk, 0.0)
        k2 = jnp.where(valid, k2, 0.0)
        v = jnp.where(valid, v, 0.0)

    row = lax.broadcasted_iota(I32, (L, L), 0)
    col = lax.broadcasted_iota(I32, (L, L), 1)
    tril_incl = (col <= row)
    tril_strict = (col < row)
    eye = jnp.where(row == col, 1.0, 0.0).astype(F32)
    cs = _mm_exact_lhs(tril_incl.astype(F32).astype(BF16), logw, 3)
    cs_l = cs[L - 1:L, :]
    e_in = jnp.exp(cs)
    e_ex = jnp.exp(cs - logw)
    e_neg = jnp.exp(-cs)
    e_end = jnp.exp(cs_l - cs)
    w_l = e_in[L - 1:L, :]
    kka = kk * a
    at = -kk * e_ex
    rt = r * e_in
    bt = kka * e_neg
    kt = k2 * e_neg
    bh = kka * e_end
    kh = k2 * e_end

    rk_sum = _mm_exact_rhs(r * k2 * r_k, bd, 2)
    bonus = rk_sum * v

    ys = []
    for h in range(H_A):
        sl = slice(h * D_HEAD, (h + 1) * D_HEAD)
        at_h, rt_h, bt_h, kt_h, bh_h, kh_h, v_h = at[:, sl], rt[:, sl], bt[:, sl], kt[:, sl], bh[:, sl], kh[:, sl], v[:, sl]
        gm = _mm(jnp.concatenate([at_h, rt_h], axis=0), jnp.concatenate([bt_h, kt_h], axis=0), NT, 1)
        a_ab = jnp.where(tril_strict, gm[:L, :L], 0.0)
        a_ak = jnp.where(tril_strict, gm[:L, L:], 0.0)
        a_rb = jnp.where(tril_incl, gm[L:, :L], 0.0)
        a_rk = jnp.where(tril_incl, gm[L:, L:], 0.0)
        tinv = _tri_inv(a_ab, L, eye, row, col, 3)
        av = _mm(a_ak, v_h, NN, 1)
        pu = _mm(tinv, jnp.concatenate([at_h, av], axis=1), NN, 3)
        p_h, u0 = pu[:, :D_HEAD], pu[:, D_HEAD:]
        y0 = _mm(jnp.concatenate([a_rk, a_rb], axis=1), jnp.concatenate([v_h, u0], axis=0), NN, 1)
        q_h = rt_h + _mm(a_rb, p_h, NN, 1)
        s_h = state_ref[h]
        y = y0 + _mm(q_h, s_h, NT, 3)
        m_h = _mm(p_h, bh_h, TN, 3)
        c_h = _mm(jnp.concatenate([u0, v_h], axis=0), jnp.concatenate([bh_h, kh_h], axis=0), TN, 3)
        state_ref[h] = s_h * w_l[:, sl] + _mm(s_h, m_h, NN, 3) + c_h
        mean = jnp.mean(y, axis=-1, keepdims=True)
        dlt = y - mean
        var = jnp.mean(dlt * dlt, axis=-1, keepdims=True)
        ys.append(dlt * lax.rsqrt(var + GN_EPS))
    yn = jnp.concatenate(ys, axis=-1)
    o_ref[...] = (yn * lnx_g + lnx_b + bonus) * g

    @pl.when(c == nc - 1)
    def _():
        wkv_out_ref[0] = state_ref[...]
        rl = t_real - 1 - (nc - 1) * L
        shift_out_ref[0] = z[rl:rl + 1, :]


def _rwkv(z_rows, shift0, wkv0, W, bsz, t_real, L):
    nc = -(-t_real // L)
    pad = lambda a: jnp.pad(a, ((0, 0), (0, ZA_W - a.shape[-1])))
    mu = pad(W['rwkv_mu'].reshape(1, C_RWKV))
    vec = jnp.concatenate([W['rwkv_w0'][None], W['rwkv_a0'][None], W['rwkv_k_k'][None], W['rwkv_k_a'][None],
                           W['rwkv_r_k'].reshape(1, C_A), W['rwkv_lnx_g'][None], W['rwkv_lnx_b'][None],
                           jnp.zeros((1, C_A), F32)], axis=0)
    shift0p = pad(shift0).reshape(bsz, 1, ZA_W)
    kern = functools.partial(_rwkv_kernel, L=L, nc=nc, t_real=t_real)
    full = lambda shape: pl.BlockSpec(shape, lambda b, c: (0,) * len(shape))
    o_a, wkv_new, shift_new = pl.pallas_call(
        kern,
        grid=(bsz, nc),
        in_specs=[pl.BlockSpec((L, ZA_W), lambda b, c: (b * nc + c, 0)),
                  pl.BlockSpec((1, 1, ZA_W), lambda b, c: (b, 0, 0)),
                  pl.BlockSpec((1, H_A, D_HEAD, D_HEAD), lambda b, c: (b, 0, 0, 0)),
                  full((1, ZA_W)), full((8, C_A)), full((D_DECAY_LORA, C_A)), full((D_AAA_LORA, C_A)),
                  full((D_GATE_LORA, C_A)), full((C_A, C_A))],
        out_specs=[pl.BlockSpec((L, C_A), lambda b, c: (b * nc + c, 0)),
                   pl.BlockSpec((1, H_A, D_HEAD, D_HEAD), lambda b, c: (b, 0, 0, 0)),
                   pl.BlockSpec((1, 1, ZA_W), lambda b, c: (b, 0, 0))],
        out_shape=[jax.ShapeDtypeStruct((bsz * nc * L, C_A), F32),
                   jax.ShapeDtypeStruct((bsz, H_A, D_HEAD, D_HEAD), F32),
                   jax.ShapeDtypeStruct((bsz, 1, ZA_W), F32)],
        scratch_shapes=[pltpu.VMEM((1, ZA_W), F32), pltpu.VMEM((H_A, D_HEAD, D_HEAD), F32)],
        compiler_params=_cp(("parallel", "arbitrary")),
        name="rwkv",
    )(z_rows, shift0p, wkv0, mu, vec, W['rwkv_w2'].astype(BF16), W['rwkv_a2'].astype(BF16),
      W['rwkv_g2'].astype(BF16), _block_indicator(C_A, D_HEAD))
    return o_a, wkv_new, shift_new[:, 0, :C_RWKV]


def _qknorm_kernel(q_ref, k_ref, gq_ref, gk_ref, bd_ref, qn_ref, kn_ref):
    bd = bd_ref[...]
    q = q_ref[...]
    k = k_ref[...]
    msq = _mm_exact_rhs(q * q, bd, 2) * (1.0 / D_HEAD)
    msk = _mm_exact_rhs(k * k, bd, 2) * (1.0 / D_HEAD)
    qn_ref[...] = q * lax.rsqrt(msq + RMS_EPS) * gq_ref[...] * (D_HEAD ** -0.5)
    kn_ref[...] = k * lax.rsqrt(msk + RMS_EPS) * gk_ref[...]


def _qknorm(z, gq, gk):
    n = z.shape[0]
    tm = min(512, n)
    tile = lambda g: jnp.tile(g.reshape(1, D_HEAD), (1, C_B // D_HEAD))
    return pl.pallas_call(
        _qknorm_kernel,
        grid=(n // tm,),
        in_specs=[pl.BlockSpec((tm, C_B), lambda i: (i, QB0 // C_B)),
                  pl.BlockSpec((tm, C_B), lambda i: (i, KB0 // C_B)),
                  pl.BlockSpec((1, C_B), lambda i: (0, 0)),
                  pl.BlockSpec((1, C_B), lambda i: (0, 0)),
                  pl.BlockSpec((C_B, C_B), lambda i: (0, 0))],
        out_specs=[pl.BlockSpec((tm, C_B), lambda i: (i, 0)), pl.BlockSpec((tm, C_B), lambda i: (i, 0))],
        out_shape=[jax.ShapeDtypeStruct((n, C_B), F32), jax.ShapeDtypeStruct((n, C_B), F32)],
        compiler_params=_cp(("parallel",)),
        name="qknorm",
    )(z, z, tile(gq), tile(gk), _block_indicator(C_B, D_HEAD))


def _lam_value(lam_ref, lam_init):
    lv = lam_ref[...]
    l1 = jnp.exp(jnp.sum(lv[0:1, :] * lv[1:2, :], axis=-1, keepdims=True))
    l2 = jnp.exp(jnp.sum(lv[2:3, :] * lv[3:4, :], axis=-1, keepdims=True))
    return l1 - l2 + lam_init


def _diff_attn_kernel(qi_tab, kj_tab, q_ref, k_ref, v_ref, lam_ref, g_ref, o_ref,
                      qs_ref, m_ref, l_ref, acc_ref, *, tq, tk, lam_init):
    p = pl.program_id(2)
    qi = qi_tab[p]
    kj = kj_tab[p]

    @pl.when(kj == 0)
    def _():
        q = q_ref[...]
        lane = lax.broadcasted_iota(I32, q.shape, 1)
        qs_ref[0:tq, :] = jnp.where(lane < D_HEAD, q, 0.0).astype(BF16)
        qs_ref[tq:2 * tq, :] = jnp.where(lane >= D_HEAD, q, 0.0).astype(BF16)
        m_ref[...] = jnp.full(m_ref.shape, NEG_BIG, F32)
        l_ref[...] = jnp.zeros(l_ref.shape, F32)
        acc_ref[...] = jnp.zeros(acc_ref.shape, F32)

    def body(masked):
        k = k_ref[...].astype(BF16)
        s = _dot(qs_ref[...], k, NT)
        if masked:
            qpos = qi * tq + lax.broadcasted_iota(I32, (tq, tk), 0)
            kpos = kj * tk + lax.broadcasted_iota(I32, (tq, tk), 1)
            ok = kpos <= qpos
            ok = jnp.concatenate([ok, ok], axis=0)
            s = jnp.where(ok, s, NEG_BIG)
        m_prev = m_ref[...]
        m_new = jnp.maximum(m_prev, jnp.max(s, axis=-1, keepdims=True))
        alpha = jnp.exp(m_prev - m_new)
        pr = jnp.exp(s - m_new)
        l_ref[...] = alpha * l_ref[...] + jnp.sum(pr, axis=-1, keepdims=True)
        acc_ref[...] = alpha * acc_ref[...] + _dot(pr.astype(BF16), v_ref[...].astype(BF16))
        m_ref[...] = m_new

    crosses = (kj + 1) * tk - 1 > qi * tq
    pl.when(crosses)(lambda: body(True))
    pl.when(jnp.logical_not(crosses))(lambda: body(False))

    @pl.when(kj == ((qi + 1) * tq - 1) // tk)
    def _():
        lam = _lam_value(lam_ref, lam_init)
        acc = acc_ref[...]
        inv_l = 1.0 / l_ref[...]
        o = acc[:tq] * inv_l[:tq] - lam * (acc[tq:] * inv_l[tq:])
        ms = jnp.mean(o * o, axis=-1, keepdims=True)
        o_ref[...] = o * lax.rsqrt(ms + RMS_EPS) * g_ref[...] * (1.0 - lam_init)


def _causal_pairs(nq, tq, tk, descending):
    qi_l, kj_l = [], []
    for qi in range(nq):
        last = ((qi + 1) * tq - 1) // tk
        ks = range(last, -1, -1) if descending else range(last + 1)
        for kj in ks:
            qi_l.append(qi)
            kj_l.append(kj)
    return jnp.asarray(qi_l, I32), jnp.asarray(kj_l, I32)


def _diff_attn(qn, kn, z, lam_w, subln_g, bsz, t_len, lam_init):
    tq = tk = min(512, t_len)
    nq, nk = t_len // tq, t_len // tk
    qi_tab, kj_tab = _causal_pairs(nq, tq, tk, False)
    vcol = VB0 // LANES
    kern = functools.partial(_diff_attn_kernel, tq=tq, tk=tk, lam_init=lam_init)
    grid_spec = pltpu.PrefetchScalarGridSpec(
        num_scalar_prefetch=2,
        grid=(bsz, H_B, int(qi_tab.shape[0])),
        in_specs=[pl.BlockSpec((tq, LANES), lambda b, h, p, qt, kt: (b * nq + qt[p], h)),
                  pl.BlockSpec((tk, LANES), lambda b, h, p, qt, kt: (b * nk + kt[p], h)),
                  pl.BlockSpec((tk, LANES), lambda b, h, p, qt, kt: (b * nk + kt[p], vcol + h)),
                  pl.BlockSpec((4, D_HEAD), lambda b, h, p, qt, kt: (0, 0)),
                  pl.BlockSpec((1, LANES), lambda b, h, p, qt, kt: (0, 0))],
        out_specs=pl.BlockSpec((tq, LANES), lambda b, h, p, qt, kt: (b * nq + qt[p], h)),
        scratch_shapes=[pltpu.VMEM((2 * tq, LANES), BF16), pltpu.VMEM((2 * tq, 1), F32),
                        pltpu.VMEM((2 * tq, 1), F32), pltpu.VMEM((2 * tq, LANES), F32)])
    return pl.pallas_call(
        kern, grid_spec=grid_spec,
        out_shape=jax.ShapeDtypeStruct((bsz * t_len, C_B), F32),
        compiler_params=_cp(("parallel", "parallel", "arbitrary")),
        name="diff_attn",
    )(qi_tab, kj_tab, qn, kn, z, lam_w, subln_g.reshape(1, 2 * D_HEAD))


def _sb_attn_kernel(qi_tab, kj_tab, q_ref, k_ref, v_ref, o_ref, qs_ref, c_ref, acc_ref, done_ref,
                    *, tq, tkb, sub):
    p = pl.program_id(2)
    qi = qi_tab[p]
    kj = kj_tab[p]
    first_kj = ((qi + 1) * tq - 1) // tkb

    @pl.when(kj == first_kj)
    def _():
        q = q_ref[...] * (D_HEAD ** -0.5)
        lane = lax.broadcasted_iota(I32, q.shape, 1)
        qs_ref[0:tq, :] = jnp.where(lane < D_HEAD, q, 0.0).astype(BF16)
        qs_ref[tq:2 * tq, :] = jnp.where(lane >= D_HEAD, q, 0.0).astype(BF16)
        c_ref[...] = jnp.zeros(c_ref.shape, F32)
        acc_ref[...] = jnp.zeros(acc_ref.shape, F32)
        done_ref[0] = 0

    rr = lax.broadcasted_iota(I32, (sub, sub), 0)
    cc = lax.broadcasted_iota(I32, (sub, sub), 1)
    upper = jnp.where(rr > cc, 1.0, 0.0).astype(BF16)

    def body(masked):
        qs = qs_ref[...]
        for sb in reversed(range(tkb // sub)):
            ksl = slice(sb * sub, (sb + 1) * sub)
            z = _dot(qs, k_ref[ksl, :].astype(BF16), NT)
            logb, l1 = _log_sigmoid_pair(z)
            if masked:
                qpos = qi * tq + lax.broadcasted_iota(I32, (tq, sub), 0)
                kpos = kj * tkb + sb * sub + lax.broadcasted_iota(I32, (tq, sub), 1)
                ok = kpos < qpos
                ok = jnp.concatenate([ok, ok], axis=0)
                l1 = jnp.where(ok, l1, 0.0)
            suf = _mm_exact_rhs(l1, upper, 2)
            c_prev = c_ref[...]
            att = jnp.exp(logb + suf + c_prev)
            if masked:
                att = jnp.where(ok, att, 0.0)
            acc_ref[...] += _dot(att.astype(BF16), v_ref[ksl, :].astype(BF16))
            c_ref[...] = c_prev + jnp.sum(l1, axis=-1, keepdims=True)
        done_ref[0] = (jnp.max(c_ref[...]) < SB_DEAD).astype(I32)

    live = done_ref[0] == 0
    crosses = (kj + 1) * tkb > qi * tq
    pl.when(jnp.logical_and(live, crosses))(lambda: body(True))
    pl.when(jnp.logical_and(live, jnp.logical_not(crosses)))(lambda: body(False))

    @pl.when(kj == 0)
    def _():
        acc = acc_ref[...]
        lane = lax.broadcasted_iota(I32, (tq, LANES), 1)
        o_ref[...] = jnp.where(lane < D_HEAD, acc[:tq], acc[tq:])


def _sb_attn(z, bsz, t_len):
    tq = tkb = min(512, t_len)
    sub = min(128, tkb)
    nq, nk = t_len // tq, t_len // tkb
    qi_tab, kj_tab = _causal_pairs(nq, tq, tkb, True)
    qc, kc, vc = QC0 // LANES, KC0 // LANES, VC0 // LANES
    kern = functools.partial(_sb_attn_kernel, tq=tq, tkb=tkb, sub=sub)
    grid_spec = pltpu.PrefetchScalarGridSpec(
        num_scalar_prefetch=2,
        grid=(bsz, H_C // 2, int(qi_tab.shape[0])),
        in_specs=[pl.BlockSpec((tq, LANES), lambda b, h, p, qt, kt: (b * nq + qt[p], qc + h)),
                  pl.BlockSpec((tkb, LANES), lambda b, h, p, qt, kt: (b * nk + kt[p], kc + h)),
                  pl.BlockSpec((tkb, LANES), lambda b, h, p, qt, kt: (b * nk + kt[p], vc + h))],
        out_specs=pl.BlockSpec((tq, LANES), lambda b, h, p, qt, kt: (b * nq + qt[p], h)),
        scratch_shapes=[pltpu.VMEM((2 * tq, LANES), BF16), pltpu.VMEM((2 * tq, 1), F32),
                        pltpu.VMEM((2 * tq, LANES), F32), pltpu.SMEM((1,), I32)])
    return pl.pallas_call(
        kern, grid_spec=grid_spec,
        out_shape=jax.ShapeDtypeStruct((bsz * t_len, C_C), F32),
        compiler_params=_cp(("parallel", "parallel", "arbitrary")),
        name="sb_attn",
    )(qi_tab, kj_tab, z, z, z)


DEC_ROWS = 16


def _diff_dec_kernel(pt_ref, q_ref, ks_ref, vs_ref, lam_ref, g_ref, *rest, G, lam_init):
    k_refs, v_refs = rest[:G], rest[G:2 * G]
    o_ref, qm_ref, m_ref, l_ref, acc_ref = rest[2 * G:]
    j = pl.program_id(1)
    hd = 2 * D_HEAD
    rowi = lax.broadcasted_iota(I32, (DEC_ROWS, hd), 0)
    lane = lax.broadcasted_iota(I32, (DEC_ROWS, hd), 1)
    sel = (lane // D_HEAD) == rowi

    def head_q(h):
        return jnp.where(sel, q_ref[0][:, h * hd:(h + 1) * hd], 0.0)

    @pl.when(j == 0)
    def _():
        for h in range(H_B):
            qm_ref[h * DEC_ROWS:(h + 1) * DEC_ROWS, :] = head_q(h).astype(BF16)
        m_ref[...] = jnp.full(m_ref.shape, NEG_BIG, F32)
        l_ref[...] = jnp.zeros(l_ref.shape, F32)
        acc_ref[...] = jnp.zeros(acc_ref.shape, F32)

    qm = qm_ref[...]
    for gi in range(G):
        s = jnp.concatenate(
            [_dot(qm[h * DEC_ROWS:(h + 1) * DEC_ROWS], k_refs[gi][pl.ds(h, PAGE_SIZE, stride=H_B), :].astype(BF16), NT)
             for h in range(H_B)], axis=0)
        m_prev = m_ref[...]
        m_new = jnp.maximum(m_prev, jnp.max(s, axis=-1, keepdims=True))
        alpha = jnp.exp(m_prev - m_new)
        pr = jnp.exp(s - m_new)
        l_ref[...] = alpha * l_ref[...] + jnp.sum(pr, axis=-1, keepdims=True)
        pr = pr.astype(BF16)
        pv = jnp.concatenate(
            [_dot(pr[h * DEC_ROWS:(h + 1) * DEC_ROWS], v_refs[gi][pl.ds(h, PAGE_SIZE, stride=H_B), :].astype(BF16))
             for h in range(H_B)], axis=0)
        acc_ref[...] = alpha * acc_ref[...] + pv
        m_ref[...] = m_new

    @pl.when(j == pl.num_programs(1) - 1)
    def _():
        lam = _lam_value(lam_ref, lam_init)
        outs = []
        for h in range(H_B):
            rs = slice(h * DEC_ROWS, (h + 1) * DEC_ROWS)
            cs = slice(h * hd, (h + 1) * hd)
            s = jnp.sum(head_q(h) * ks_ref[0][:, cs], axis=-1, keepdims=True)
            m_prev = m_ref[rs, :]
            m_new = jnp.maximum(m_prev, s)
            alpha = jnp.exp(m_prev - m_new)
            pr = jnp.exp(s - m_new)
            l_fin = alpha * l_ref[rs, :] + pr
            att = (alpha * acc_ref[rs, :] + pr * vs_ref[0][:, cs]) / l_fin
            o = att[0:1, :] - lam * att[1:2, :]
            ms = jnp.mean(o * o, axis=-1, keepdims=True)
            outs.append(o * lax.rsqrt(ms + RMS_EPS))
        o_ref[0] = jnp.concatenate(outs, axis=-1) * g_ref[...] * (1.0 - lam_init)


def _pages_per_step(n_pages):
    g = 8
    while n_pages % g:
        g //= 2
    return g


def _diff_dec(qn, kn, vb, cache_k, cache_v, page_table, layer, lam_w, subln_g, lam_init):
    bsz, n_pages = page_table.shape
    G = _pages_per_step(n_pages)
    nl = cache_k.shape[1]
    hd = 2 * D_HEAD
    ck = cache_k.reshape(cache_k.shape[0], nl, PAGE_SIZE * H_B, hd)
    cv = cache_v.reshape(cache_v.shape[0], nl, PAGE_SIZE * H_B, hd)
    row3 = lambda a: a.reshape(bsz, 1, C_B)
    vec_spec = pl.BlockSpec((1, 1, C_B), lambda b, j, pt: (b, 0, 0))
    page_spec = lambda gi: pl.BlockSpec((None, None, PAGE_SIZE * H_B, hd),
                                        lambda b, j, pt: (pt[b, j * G + gi], layer, 0, 0))
    kern = functools.partial(_diff_dec_kernel, G=G, lam_init=lam_init)
    grid_spec = pltpu.PrefetchScalarGridSpec(
        num_scalar_prefetch=1,
        grid=(bsz, n_pages // G),
        in_specs=[vec_spec, vec_spec, vec_spec,
                  pl.BlockSpec((4, D_HEAD), lambda b, j, pt: (0, 0)),
                  pl.BlockSpec((1, C_B), lambda b, j, pt: (0, 0))]
                 + [page_spec(gi) for gi in range(G)] + [page_spec(gi) for gi in range(G)],
        out_specs=pl.BlockSpec((1, 1, C_B), lambda b, j, pt: (b, 0, 0)),
        scratch_shapes=[pltpu.VMEM((H_B * DEC_ROWS, hd), BF16), pltpu.VMEM((H_B * DEC_ROWS, 1), F32),
                        pltpu.VMEM((H_B * DEC_ROWS, 1), F32), pltpu.VMEM((H_B * DEC_ROWS, hd), F32)])
    out = pl.pallas_call(
        kern, grid_spec=grid_spec,
        out_shape=jax.ShapeDtypeStruct((bsz, 1, C_B), F32),
        compiler_params=_cp(("parallel", "arbitrary")),
        name="diff_dec",
    )(page_table, row3(qn), row3(kn), row3(vb), lam_w,
      jnp.tile(subln_g.reshape(1, 2 * D_HEAD), (1, H_B)), *([ck] * G), *([cv] * G))
    return out.reshape(bsz, C_B)


def _sb_dec_kernel(pt_ref, q_ref, *rest, G):
    k_refs, v_refs = rest[:G], rest[G:2 * G]
    o_ref, qm_ref, c_ref, acc_ref, done_ref = rest[2 * G:]
    j = pl.program_id(1)
    shape = (DEC_ROWS, C_C)
    rowi = lax.broadcasted_iota(I32, shape, 0)
    lane = lax.broadcasted_iota(I32, shape, 1)
    sel = (lane // D_HEAD) == rowi

    @pl.when(j == 0)
    def _():
        qm_ref[...] = jnp.where(sel, q_ref[0] * (D_HEAD ** -0.5), 0.0).astype(BF16)
        c_ref[...] = jnp.zeros(c_ref.shape, F32)
        acc_ref[...] = jnp.zeros(acc_ref.shape, F32)
        done_ref[0] = 0

    @pl.when(done_ref[0] == 0)
    def _():
        rr = lax.broadcasted_iota(I32, (PAGE_SIZE, PAGE_SIZE), 0)
        cc = lax.broadcasted_iota(I32, (PAGE_SIZE, PAGE_SIZE), 1)
        upper = jnp.where(rr > cc, 1.0, 0.0).astype(BF16)
        qm = qm_ref[...]
        for gi in range(G):
            z = _dot(qm, k_refs[gi][...].astype(BF16))
            logb, l1 = _log_sigmoid_pair(z)
            suf = _mm_exact_rhs(l1, upper, 2)
            c_prev = c_ref[...]
            att = jnp.exp(logb + suf + c_prev)
            acc_ref[...] += _dot(att.astype(BF16), v_refs[gi][...].astype(BF16), NT)
            c_ref[...] = c_prev + jnp.sum(l1, axis=-1, keepdims=True)
        row1 = lax.broadcasted_iota(I32, (DEC_ROWS, 1), 0)
        cmax = jnp.max(jnp.where(row1 < H_C, c_ref[...], NEG_BIG))
        done_ref[0] = (cmax < SB_DEAD).astype(I32)

    @pl.when(j == pl.num_programs(1) - 1)
    def _():
        o_ref[0] = jnp.sum(jnp.where(sel, acc_ref[...], 0.0), axis=0, keepdims=True)


def _sb_dec(qc, cache_k, cache_v, page_table, layer):
    bsz, n_pages = page_table.shape
    G = _pages_per_step(n_pages)
    nl = cache_k.shape[1]
    ck = jnp.transpose(cache_k, (0, 1, 3, 4, 2)).reshape(cache_k.shape[0], nl, C_C, PAGE_SIZE)
    cv = jnp.transpose(cache_v, (0, 1, 3, 4, 2)).reshape(cache_v.shape[0], nl, C_C, PAGE_SIZE)
    page_spec = lambda gi: pl.BlockSpec((None, None, C_C, PAGE_SIZE),
                                        lambda b, j, pt: (pt[b, n_pages - 1 - (j * G + gi)], layer, 0, 0))
    kern = functools.partial(_sb_dec_kernel, G=G)
    grid_spec = pltpu.PrefetchScalarGridSpec(
        num_scalar_prefetch=1,
        grid=(bsz, n_pages // G),
        in_specs=[pl.BlockSpec((1, 1, C_C), lambda b, j, pt: (b, 0, 0))]
                 + [page_spec(gi) for gi in range(G)] + [page_spec(gi) for gi in range(G)],
        out_specs=pl.BlockSpec((1, 1, C_C), lambda b, j, pt: (b, 0, 0)),
        scratch_shapes=[pltpu.VMEM((DEC_ROWS, C_C), BF16), pltpu.VMEM((DEC_ROWS, 1), F32),
                        pltpu.VMEM((DEC_ROWS, C_C), F32), pltpu.SMEM((1,), I32)])
    out = pl.pallas_call(
        kern, grid_spec=grid_spec,
        out_shape=jax.ShapeDtypeStruct((bsz, 1, C_C), F32),
        compiler_params=_cp(("parallel", "arbitrary")),
        name="sb_dec",
    )(page_table, qc.reshape(bsz, 1, C_C), *([ck] * G), *([cv] * G))
    return out.reshape(bsz, C_C)


def _merge_kernel(h_ref, oa_ref, ob_ref, oc_ref, g0_ref, g1_ref, g2_ref, wa_ref, wb_ref, wc_ref, wo_ref,
                  gn_ref, rw_ref, rb_ref, h_out_ref, xn_ref, ridx_ref, rgate_ref):
    mix = (_sigmoid(g0_ref[...]) * _dot(oa_ref[...].astype(BF16), wa_ref[...])
           + _sigmoid(g1_ref[...]) * _dot(ob_ref[...].astype(BF16), wb_ref[...])
           + _sigmoid(g2_ref[...]) * _dot(oc_ref[...].astype(BF16), wc_ref[...]))
    h = h_ref[...] + _dot(mix.astype(BF16), wo_ref[...])
    h_out_ref[...] = h
    ms = jnp.mean(h * h, axis=-1, keepdims=True)
    xn = h * lax.rsqrt(ms + RMS_EPS) * gn_ref[...]
    xn_ref[...] = xn
    logits = _mm(xn, rw_ref[...], NN, 3) + rb_ref[...]
    tm = logits.shape[0]
    lane = lax.broadcasted_iota(I32, logits.shape, 1)
    lane_o = lax.broadcasted_iota(I32, (tm, LANES), 1)
    work = logits
    vals, idxs = [], []
    for _ in range(TOP_K):
        mx = jnp.max(work, axis=-1, keepdims=True)
        ik = jnp.min(jnp.where(work == mx, lane, N_EXPERTS), axis=-1, keepdims=True)
        vals.append(mx)
        idxs.append(ik)
        work = jnp.where(lane == ik, -jnp.inf, work)
    es = [jnp.exp(vk - vals[0]) for vk in vals]
    den = es[0] + es[1] + es[2] + es[3]
    ridx = jnp.zeros((tm, LANES), I32)
    rgate = jnp.zeros((tm, LANES), F32)
    for kx in range(TOP_K):
        ridx = jnp.where(lane_o == kx, idxs[kx], ridx)
        rgate = jnp.where(lane_o == kx, es[kx] / den, rgate)
    ridx_ref[...] = ridx
    rgate_ref[...] = rgate


def _merge(h, o_a, o_b, o_c, z, W):
    n, d = h.shape
    tm = min(512, n)
    gcol = ZG0 // d
    row = lambda c: pl.BlockSpec((tm, c), lambda i: (i, 0))
    full = lambda a: pl.BlockSpec(a.shape, lambda i: (0,) * a.ndim)
    wa, wb, wc, wo = (W['w_branch_a'].astype(BF16), W['w_branch_b'].astype(BF16),
                      W['w_branch_c'].astype(BF16), W['w_out'].astype(BF16))
    gn = W['norm_ffn_g'].reshape(1, d)
    rw = W['router_w']
    rb = W['router_b'].reshape(1, N_EXPERTS)
    return pl.pallas_call(
        _merge_kernel,
        grid=(n // tm,),
        in_specs=[row(d), row(C_A), row(C_B), row(C_C),
                  pl.BlockSpec((tm, d), lambda i: (i, gcol)),
                  pl.BlockSpec((tm, d), lambda i: (i, gcol + 1)),
                  pl.BlockSpec((tm, d), lambda i: (i, gcol + 2)),
                  full(wa), full(wb), full(wc), full(wo), full(gn), full(rw), full(rb)],
        out_specs=[row(d), row(d), row(LANES), row(LANES)],
        out_shape=[jax.ShapeDtypeStruct((n, d), F32), jax.ShapeDtypeStruct((n, d), F32),
                   jax.ShapeDtypeStruct((n, LANES), I32), jax.ShapeDtypeStruct((n, LANES), F32)],
        compiler_params=_cp(("parallel",), VMEM_LIMIT),
        name="merge",
    )(h, o_a, o_b, o_c, z, z, z, wa, wb, wc, wo, gn, rw, rb)


def _route_kernel(ridx_ref, rpos_ref, cnt_ref, base_ref):
    i = pl.program_id(0)

    @pl.when(i == 0)
    def _():
        base_ref[...] = jnp.zeros(base_ref.shape, F32)

    ridx = ridx_ref[...]
    tm = ridx.shape[0]
    lane_e = lax.broadcasted_iota(I32, (tm, N_EXPERTS), 1)
    lane_o = lax.broadcasted_iota(I32, (tm, LANES), 1)
    onehots = [(lane_e == ridx[:, kx:kx + 1]).astype(F32) for kx in range(TOP_K)]
    sel = onehots[0] + onehots[1] + onehots[2] + onehots[3]
    rr = lax.broadcasted_iota(I32, (tm, tm), 0)
    cc = lax.broadcasted_iota(I32, (tm, tm), 1)
    before = jnp.where(cc < rr, 1.0, 0.0).astype(BF16)
    rank = _dot(before, sel.astype(BF16)) + base_ref[...]
    rpos = jnp.zeros((tm, LANES), I32)
    for kx in range(TOP_K):
        pk = jnp.sum(onehots[kx] * rank, axis=-1, keepdims=True)
        rpos = jnp.where(lane_o == kx, pk.astype(I32), rpos)
    rpos_ref[...] = rpos
    base_ref[...] = base_ref[...] + jnp.sum(sel, axis=0, keepdims=True)
    cnt_ref[...] = base_ref[...]


def _route(ridx):
    n = ridx.shape[0]
    tm = min(256, n)
    return pl.pallas_call(
        _route_kernel,
        grid=(n // tm,),
        in_specs=[pl.BlockSpec((tm, LANES), lambda i: (i, 0))],
        out_specs=[pl.BlockSpec((tm, LANES), lambda i: (i, 0)), pl.BlockSpec((1, N_EXPERTS), lambda i: (0, 0))],
        out_shape=[jax.ShapeDtypeStruct((n, LANES), I32), jax.ShapeDtypeStruct((1, N_EXPERTS), F32)],
        scratch_shapes=[pltpu.VMEM((1, N_EXPERTS), F32)],
        compiler_params=_cp(("arbitrary",)),
        name="route",
    )(ridx)


def _row_copy(src, src_row, dst, dst_row, sem):
    return pltpu.make_async_copy(src.at[pl.ds(pl.multiple_of(src_row * ROW_TILES, ROW_TILES), ROW_TILES)],
                                 dst.at[pl.ds(pl.multiple_of(dst_row * ROW_TILES, ROW_TILES), ROW_TILES)], sem)


def _dispatch_kernel(dest_ref, tail_ref, nused_ref, x_ref, xs_ref, zbuf_ref, sem_z, sem, *, tm, blk, n_blk):
    i = pl.program_id(0)

    def zero_copy(row0):
        return pltpu.make_async_copy(
            zbuf_ref, xs_ref.at[pl.ds(pl.multiple_of(row0 * ROW_TILES, ROW_TILES), blk * ROW_TILES)], sem_z)

    @pl.when(i == 0)
    def _():
        zbuf_ref[...] = jnp.zeros(zbuf_ref.shape, F32)
        for e in range(N_EXPERTS):
            pl.when(tail_ref[e] >= 0)(lambda e=e: zero_copy(tail_ref[e]).start())
        for e in range(N_EXPERTS):
            pl.when(tail_ref[e] >= 0)(lambda e=e: zero_copy(tail_ref[e]).wait())

        def fill(b, carry):
            cp = zero_copy(b * blk)
            cp.start()
            cp.wait()
            return carry

        lax.fori_loop(nused_ref[0], n_blk, fill, 0)

    def issue(r, carry):
        for kx in range(TOP_K):
            _row_copy(x_ref, r, xs_ref, dest_ref[r * TOP_K + kx], sem).start()
        return carry

    lax.fori_loop(0, tm, issue, 0)

    def drain(r, carry):
        for kx in range(TOP_K):
            _row_copy(x_ref, 0, xs_ref, 0, sem).wait()
        return carry

    lax.fori_loop(0, tm, drain, 0)


def _dispatch(xn_tiles, dest_flat, tail_rows, nused, n_blk, blk):
    n = xn_tiles.shape[0] // ROW_TILES
    tm = min(256, n)
    kern = functools.partial(_dispatch_kernel, tm=tm, blk=blk, n_blk=n_blk)
    return pl.pallas_call(
        kern,
        grid=(n // tm,),
        in_specs=[pl.BlockSpec((tm * TOP_K,), lambda i: (i,), memory_space=pltpu.SMEM),
                  pl.BlockSpec(memory_space=pltpu.SMEM),
                  pl.BlockSpec(memory_space=pltpu.SMEM),
                  pl.BlockSpec((tm * ROW_TILES, LANES), lambda i: (i, 0))],
        out_specs=pl.BlockSpec(memory_space=pl.ANY),
        out_shape=jax.ShapeDtypeStruct((n_blk * blk * ROW_TILES, LANES), F32),
        scratch_shapes=[pltpu.VMEM((blk * ROW_TILES, LANES), F32),
                        pltpu.SemaphoreType.DMA(()), pltpu.SemaphoreType.DMA(())],
        compiler_params=_cp(("arbitrary",)),
        name="moe_dispatch",
    )(dest_flat, tail_rows, nused, xn_tiles)


def _rows_from_tiles(ref, n):
    return jnp.concatenate([ref[pl.ds(s, n, stride=ROW_TILES), :] for s in range(ROW_TILES)], axis=1)


def _expert_kernel(blk_e, nused, x_ref, wgu_ref, bgu_ref, wd_ref, bd_ref, y_ref, wgu_bf, wd_bf, *, blk, d_ff):
    i = pl.program_id(0)

    @pl.when(i < nused[0])
    def _():
        e = blk_e[i]
        e_prev = blk_e[jnp.maximum(i - 1, 0)]

        @pl.when(jnp.logical_or(i == 0, e != e_prev))
        def _():
            wgu_bf[...] = wgu_ref[...].astype(BF16)
            wd_bf[...] = wd_ref[...].astype(BF16)

        x = _rows_from_tiles(x_ref, blk).astype(BF16)
        hh = _dot(x, wgu_bf[...]) + bgu_ref[...]
        hg = jnp.minimum(hh[:, :d_ff], SWIGLU_LIMIT)
        hl = jnp.clip(hh[:, d_ff:], -SWIGLU_LIMIT, SWIGLU_LIMIT)
        act = hg * _sigmoid(SWIGLU_ALPHA * hg) * (hl + 1.0)
        y = _dot(act.astype(BF16), wd_bf[...]) + bd_ref[...]
        for s in range(ROW_TILES):
            y_ref[pl.ds(s, blk, stride=ROW_TILES), :] = y[:, s * LANES:(s + 1) * LANES]

    @pl.when(i >= nused[0])
    def _():
        y_ref[...] = jnp.zeros(y_ref.shape, F32)


def _experts(xs, blk_e, nused, w_gu, b_gu, w_down, b_down, blk):
    n_blk = blk_e.shape[0]
    n_e, d, f2 = w_gu.shape
    d_ff = f2 // 2
    kern = functools.partial(_expert_kernel, blk=blk, d_ff=d_ff)
    last = lambda i, be, nu: jnp.minimum(i, nu[0] - 1)
    grid_spec = pltpu.PrefetchScalarGridSpec(
        num_scalar_prefetch=2,
        grid=(n_blk,),
        in_specs=[pl.BlockSpec((blk * ROW_TILES, LANES), lambda i, be, nu: (last(i, be, nu), 0)),
                  pl.BlockSpec((None, d, f2), lambda i, be, nu: (be[i], 0, 0)),
                  pl.BlockSpec((None, 1, f2), lambda i, be, nu: (be[i], 0, 0)),
                  pl.BlockSpec((None, d_ff, d), lambda i, be, nu: (be[i], 0, 0)),
                  pl.BlockSpec((None, 1, d), lambda i, be, nu: (be[i], 0, 0))],
        out_specs=pl.BlockSpec((blk * ROW_TILES, LANES), lambda i, be, nu: (i, 0)),
        scratch_shapes=[pltpu.VMEM((d, f2), BF16), pltpu.VMEM((d_ff, d), BF16)])
    return pl.pallas_call(
        kern, grid_spec=grid_spec,
        out_shape=jax.ShapeDtypeStruct(xs.shape, F32),
        compiler_params=_cp(("arbitrary",), VMEM_LIMIT),
        name="moe_experts",
    )(blk_e, nused, xs, w_gu, b_gu.reshape(n_e, 1, f2), w_down, b_down.reshape(n_e, 1, d))


def _combine_kernel(dest_ref, gate_ref, h_ref, pe_ref, yb_ref, gp_ref, pw_ref, pg_ref, o_ref, buf_ref, sem, *, tm):
    def issue(r, carry):
        for kx in range(TOP_K):
            _row_copy(yb_ref, dest_ref[r * TOP_K + kx], buf_ref.at[kx], r, sem).start()
        return carry

    lax.fori_loop(0, tm, issue, 0)

    def drain(r, carry):
        for kx in range(TOP_K):
            _row_copy(yb_ref, 0, buf_ref.at[kx], 0, sem).wait()
        return carry

    lax.fori_loop(0, tm, drain, 0)

    gate = gate_ref[...]
    h = h_ref[...]
    for kx in range(TOP_K):
        h = h + gate[:, kx:kx + 1] * _rows_from_tiles(buf_ref.at[kx], tm)
    ms = jnp.mean(h * h, axis=-1, keepdims=True)
    xn = (h * lax.rsqrt(ms + RMS_EPS) * gp_ref[...]).astype(BF16)
    o_ref[...] = h + _dot(pe_ref[...].astype(BF16), pw_ref[...]) * _sigmoid(_dot(xn, pg_ref[...]))


def _combine(h, pe, yb, dest_flat, rgate, W):
    n, d = h.shape
    tm = min(256, n)
    gp = W['norm_ple_g'].reshape(1, d)
    pw = W['ple_w'].astype(BF16)
    pg = W['ple_gate_w'].astype(BF16)
    full = lambda a: pl.BlockSpec(a.shape, lambda i: (0,) * a.ndim)
    kern = functools.partial(_combine_kernel, tm=tm)
    return pl.pallas_call(
        kern,
        grid=(n // tm,),
        in_specs=[pl.BlockSpec((tm * TOP_K,), lambda i: (i,), memory_space=pltpu.SMEM),
                  pl.BlockSpec((tm, LANES), lambda i: (i, 0)),
                  pl.BlockSpec((tm, d), lambda i: (i, 0)),
                  pl.BlockSpec((tm, pe.shape[1]), lambda i: (i, 0)),
                  pl.BlockSpec(memory_space=pl.ANY),
                  full(gp), full(pw), full(pg)],
        out_specs=pl.BlockSpec((tm, d), lambda i: (i, 0)),
        out_shape=jax.ShapeDtypeStruct((n, d), F32),
        scratch_shapes=[pltpu.VMEM((TOP_K, tm * ROW_TILES, LANES), F32), pltpu.SemaphoreType.DMA(())],
        compiler_params=_cp(("arbitrary",)),
        name="moe_combine",
    )(dest_flat, rgate, h, pe, yb, gp, pw, pg)


def _moe_and_ple(h, xn, ridx, rgate, pe, W):
    n, d = h.shape
    m = n * TOP_K
    blk = max(16, min(512, (m // N_EXPERTS) // 16 * 16))
    n_blk = -(-m // blk) + N_EXPERTS
    rpos, counts = _route(ridx)
    counts = counts[0].astype(I32)
    padded = (counts + blk - 1) // blk * blk
    pad_end = jnp.cumsum(padded)
    pad_start = pad_end - padded
    dest = pad_start[ridx[:, :TOP_K]] + rpos[:, :TOP_K]
    dest_flat = dest.reshape(m).astype(I32)
    blk_e = jnp.minimum(jnp.searchsorted(pad_end, jnp.arange(n_blk, dtype=I32) * blk, side='right'),
                        N_EXPERTS - 1).astype(I32)
    nused = (pad_end[-1] // blk).astype(I32).reshape(1)
    blk_e = jnp.where(jnp.arange(n_blk) < nused[0], blk_e, blk_e[jnp.maximum(nused[0] - 1, 0)])
    tail_rows = jnp.where(counts % blk != 0, pad_end - blk, -1).astype(I32)
    xn_tiles = xn.reshape(n * ROW_TILES, LANES)
    xs = _dispatch(xn_tiles, dest_flat, tail_rows, nused, n_blk, blk)
    yb = _experts(xs, blk_e, nused, W['moe_w_gu'], W['moe_b_gu'], W['moe_w_down'], W['moe_b_down'], blk)
    return _combine(h, pe, yb, dest_flat, rgate, W)


def _layer_common(h, z, o_a, o_b, o_c, pe, W):
    h_mid, xn, ridx, rgate = _merge(h, o_a, o_b, o_c, z, W)
    return _moe_and_ple(h_mid, xn, ridx, rgate, pe, W)


def _layer_prompt(h, pe, wp, W, bsz, t_len, lam_init):
    n = bsz * t_len
    z = _in_proj(h, W['norm_mix_g'], wp)
    L = min(64, t_len)
    shift0 = jnp.zeros((bsz, C_RWKV), F32)
    wkv0 = jnp.zeros((bsz, H_A, D_HEAD, D_HEAD), F32)
    o_a, wkv_new, shift_new = _rwkv(z, shift0, wkv0, W, bsz, t_len, L)
    qn, kn = _qknorm(z, W['diff_q_norm'], W['diff_k_norm'])
    o_b = _diff_attn(qn, kn, z, W['diff_lambda'], W['diff_subln_g'], bsz, t_len, lam_init)
    o_c = _sb_attn(z, bsz, t_len)
    h = _layer_common(h, z, o_a, o_b, o_c, pe, W)
    rows = (kn.reshape(bsz, t_len, H_B, 2 * D_HEAD),
            z[:, VB0:VB0 + C_B].reshape(bsz, t_len, H_B, 2 * D_HEAD),
            z[:, KC0:KC0 + C_C].reshape(bsz, t_len, H_C, D_HEAD),
            z[:, VC0:VC0 + C_C].reshape(bsz, t_len, H_C, D_HEAD),
            wkv_new, shift_new)
    return h, rows


DEC_CHUNK = 16


def _layer_sample(h, pe, wp, W, caches, page_table, shift0, wkv0, layer, lam_init):
    bsz = h.shape[0]
    z = _in_proj(h, W['norm_mix_g'], wp)
    za = jnp.pad(z[:, None, :ZA_W], ((0, 0), (0, DEC_CHUNK - 1), (0, 0))).reshape(bsz * DEC_CHUNK, ZA_W)
    o_a, wkv_new, shift_new = _rwkv(za, shift0, wkv0, W, bsz, 1, DEC_CHUNK)
    o_a = o_a.reshape(bsz, DEC_CHUNK, C_A)[:, 0]
    qn, kn = _qknorm(z, W['diff_q_norm'], W['diff_k_norm'])
    vb = z[:, VB0:VB0 + C_B]
    o_b = _diff_dec(qn, kn, vb, caches[0], caches[1], page_table, layer, W['diff_lambda'], W['diff_subln_g'], lam_init)
    o_c = _sb_dec(z[:, QC0:QC0 + C_C], caches[2], caches[3], page_table, layer)
    h = _layer_common(h, z, o_a, o_b, o_c, pe, W)
    rows = (kn.reshape(bsz, 1, H_B, 2 * D_HEAD), vb.reshape(bsz, 1, H_B, 2 * D_HEAD),
            z[:, KC0:KC0 + C_C].reshape(bsz, 1, H_C, D_HEAD), z[:, VC0:VC0 + C_C].reshape(bsz, 1, H_C, D_HEAD),
            wkv_new, shift_new)
    return h, rows


def kernel(x_prompt, x_sample, cache_diff_k, cache_diff_v, cache_sb_k, cache_sb_v, state_wkv, state_shift, page_table, p_prompt, p_sample, norm_mix_g, w_in, rwkv_mu, rwkv_w0, rwkv_w2, rwkv_a0, rwkv_a2, rwkv_g2, rwkv_k_k, rwkv_k_a, rwkv_r_k, rwkv_lnx_g, rwkv_lnx_b, diff_q_norm, diff_k_norm, diff_lambda, diff_subln_g, w_branch_a, w_branch_b, w_branch_c, w_out, norm_ffn_g, router_w, router_b, moe_w_gu, moe_b_gu, moe_w_down, moe_b_down, norm_ple_g, ple_w, ple_gate_w):
    bsz_p, seq_p, d = x_prompt.shape
    bsz_s, seq_s, _ = x_sample.shape
    assert seq_s == 1 and d == ROW_TILES * LANES
    depth = w_in.shape[0]
    params = dict(norm_mix_g=norm_mix_g, rwkv_mu=rwkv_mu, rwkv_w0=rwkv_w0, rwkv_w2=rwkv_w2, rwkv_a0=rwkv_a0,
                  rwkv_a2=rwkv_a2, rwkv_g2=rwkv_g2, rwkv_k_k=rwkv_k_k, rwkv_k_a=rwkv_k_a, rwkv_r_k=rwkv_r_k,
                  rwkv_lnx_g=rwkv_lnx_g, rwkv_lnx_b=rwkv_lnx_b, diff_q_norm=diff_q_norm, diff_k_norm=diff_k_norm,
                  diff_lambda=diff_lambda, diff_subln_g=diff_subln_g, w_branch_a=w_branch_a,
                  w_branch_b=w_branch_b, w_branch_c=w_branch_c, w_out=w_out, norm_ffn_g=norm_ffn_g,
                  router_w=router_w, router_b=router_b, moe_w_gu=moe_w_gu, moe_b_gu=moe_b_gu,
                  moe_w_down=moe_w_down, moe_b_down=moe_b_down, norm_ple_g=norm_ple_g, ple_w=ple_w,
                  ple_gate_w=ple_gate_w)
    caches = (cache_diff_k, cache_diff_v, cache_sb_k, cache_sb_v)
    h_p = x_prompt.reshape(bsz_p * seq_p, d)
    h_s = x_sample.reshape(bsz_s, d)
    rows_p, rows_s = [], []
    for i in range(depth):
        W = {name: val[i] for name, val in params.items()}
        wp = _pack_w_in(w_in[i])
        lam_init = 0.8 - 0.6 * math.exp(-0.3 * i)
        h_p, r_p = _layer_prompt(h_p, p_prompt[i].reshape(bsz_p * seq_p, -1), wp, W, bsz_p, seq_p, lam_init)
        h_s, r_s = _layer_sample(h_s, p_sample[i].reshape(bsz_s, -1), wp, W, caches, page_table,
                                 state_shift[:, i], state_wkv[:, i], i, lam_init)
        rows_p.append(r_p)
        rows_s.append(r_s)
    st = lambda rows, j: jnp.stack([r[j] for r in rows], axis=1)
    return (h_p.reshape(bsz_p, seq_p, d), h_s.reshape(bsz_s, seq_s, d),
            st(rows_p, 0), st(rows_p, 1), st(rows_p, 2), st(rows_p, 3), st(rows_p, 4), st(rows_p, 5),
            st(rows_s, 0), st(rows_s, 1), st(rows_s, 2), st(rows_s, 3), st(rows_s, 4), st(rows_s, 5))
```

```python
import functools
import math

import numpy as np
import jax
import jax.numpy as jnp
from jax import lax
from jax.experimental import pallas as pl
from jax.experimental.pallas import tpu as pltpu

F32 = jnp.float32
BF16 = jnp.bfloat16
I32 = jnp.int32

D_HEAD = 64
H_A, H_B, H_C = 8, 4, 6
C_A, C_B, C_C = H_A * D_HEAD, H_B * 2 * D_HEAD, H_C * D_HEAD
D_DECAY_LORA, D_AAA_LORA, D_GATE_LORA = 64, 64, 128
C_RWKV = 3 * C_A + D_DECAY_LORA + D_AAA_LORA + D_GATE_LORA
N_EXPERTS, TOP_K = 32, 4
SWIGLU_LIMIT, SWIGLU_ALPHA = 7.0, 1.702
PAGE_SIZE = 128
RMS_EPS = 1e-6
GN_EPS = 64e-5

LANES = 128
SUBLANES = 8
ROW_TILES = 8

ZA0, ZA_W = 0, 2048
QB0, KB0, VB0 = 2048, 2560, 3072
QC0, KC0, VC0 = 3584, 4096, 4608
ZG0 = 5120
ZW = 8192

SB_DEAD = -110.0
NEG_BIG = -1e30

NN = (((1,), (0,)), ((), ()))
NT = (((1,), (1,)), ((), ()))
TN = (((0,), (0,)), ((), ()))

VMEM_LIMIT = 56 * 1024 * 1024


def _cp(sem, vmem=None):
    return pltpu.CompilerParams(dimension_semantics=sem, vmem_limit_bytes=vmem)


def _dot(a, b, dims=NN):
    return lax.dot_general(a, b, dims, preferred_element_type=F32)


def _split(x):
    hi = x.astype(BF16)
    lo = (x - hi.astype(F32)).astype(BF16)
    return hi, lo


def _mm(a, b, dims=NN, passes=1):
    if passes == 1:
        return _dot(a.astype(BF16), b.astype(BF16), dims)
    ah, al = _split(a)
    bh, bl = _split(b)
    return _dot(ah, bh, dims) + (_dot(ah, bl, dims) + _dot(al, bh, dims))


def _mm_exact_rhs(a, b_exact, passes=2):
    hi = a.astype(BF16)
    out = _dot(hi, b_exact)
    rem = a - hi.astype(F32)
    for _ in range(passes - 1):
        part = rem.astype(BF16)
        out = out + _dot(part, b_exact)
        rem = rem - part.astype(F32)
    return out


def _mm_exact_lhs(a_exact, b, passes=2):
    hi = b.astype(BF16)
    out = _dot(a_exact, hi)
    rem = b - hi.astype(F32)
    for _ in range(passes - 1):
        part = rem.astype(BF16)
        out = out + _dot(a_exact, part)
        rem = rem - part.astype(F32)
    return out


def _sigmoid(x):
    return 1.0 / (1.0 + jnp.exp(-x))


def _log_sigmoid_pair(z):
    lg = jnp.log(1.0 + jnp.exp(-jnp.abs(z)))
    return jnp.minimum(z, 0.0) - lg, -jnp.maximum(z, 0.0) - lg


def _block_indicator(n, blk):
    idx = np.arange(n) // blk
    return jnp.asarray((idx[:, None] == idx[None, :]).astype(np.float32), dtype=BF16)


def _in_proj_kernel(x_ref, g_ref, w_ref, o_ref, xn_ref):
    @pl.when(pl.program_id(1) == 0)
    def _():
        x = x_ref[...]
        ms = jnp.mean(x * x, axis=-1, keepdims=True)
        xn_ref[...] = (x * lax.rsqrt(ms + RMS_EPS) * g_ref[...]).astype(BF16)

    o_ref[...] = _dot(xn_ref[...], w_ref[...])


def _in_proj(x, g, wp):
    n, d = x.shape
    c = wp.shape[1]
    tm = min(1024, n)
    tn = 512
    return pl.pallas_call(
        _in_proj_kernel,
        grid=(n // tm, c // tn),
        in_specs=[pl.BlockSpec((tm, d), lambda i, j: (i, 0)),
                  pl.BlockSpec((1, d), lambda i, j: (0, 0)),
                  pl.BlockSpec((d, tn), lambda i, j: (0, j))],
        out_specs=pl.BlockSpec((tm, tn), lambda i, j: (i, j)),
        out_shape=jax.ShapeDtypeStruct((n, c), F32),
        scratch_shapes=[pltpu.VMEM((tm, d), BF16)],
        compiler_params=_cp(("parallel", "arbitrary")),
        name="in_proj",
    )(x, g.reshape(1, d), wp)


def _pack_w_in(w):
    c_diff, c_sb = 3 * C_B, 3 * C_C
    za = w[:, :C_RWKV]
    zb = w[:, C_RWKV:C_RWKV + c_diff]
    zc = w[:, C_RWKV + c_diff:C_RWKV + c_diff + c_sb]
    zg = w[:, C_RWKV + c_diff + c_sb:]
    pad = lambda a, n: jnp.pad(a, ((0, 0), (0, n - a.shape[1])))
    parts = [pad(za, ZA_W), zb,
             pad(zc[:, :C_C], 512), pad(zc[:, C_C:2 * C_C], 512), pad(zc[:, 2 * C_C:], 512), zg]
    return jnp.concatenate(parts, axis=1).astype(BF16)


RWKV_GROUP = 4
RWKV_PASSES = dict(tinv=1, pu=1, out=1, state=3)
RWKV_SEQS_PER_STEP = 2


def _bd(y, mask):
    yb = y.astype(BF16)
    return jnp.concatenate([yb] * RWKV_GROUP, axis=0) * mask


def _mm_bd(x, y, mask, passes, dims=NN):
    if passes == 1:
        return _dot(x.astype(BF16), _bd(y, mask), dims)
    xh, xl = _split(x)
    yh, yl = _split(y)
    bh, bl = _bd(yh, mask), _bd(yl, mask)
    return _dot(xh, bh, dims) + (_dot(xh, bl, dims) + _dot(xl, bh, dims))


def _tri_inv_cat(a, n, eye, same_blk, mask, passes):
    bs = min(16, n)
    if n > bs:
        ad = jnp.where(same_blk, a, 0.0)
        ao = a - ad
    else:
        ad, ao = a, None
    td = eye + ad
    pw = ad
    p = 2
    while p < bs:
        pw = _mm_bd(pw, pw, mask, passes)
        td = td + _mm_bd(td, pw, mask, passes)
        p *= 2
    if ao is None:
        return td
    nmat = _mm_bd(td, ao, mask, passes)
    res = eye + nmat
    npw = nmat
    p = 2
    while p < n // bs:
        npw = _mm_bd(npw, npw, mask, passes)
        res = res + _mm_bd(res, npw, mask, passes)
        p *= 2
    return _mm_bd(res, td, mask, passes)


def _rwkv_seq(bi, c, z_ref, mu_ref, vec_ref, w2_ref, a2_ref, g2_ref, bd_ref, mch_ref, mll_ref,
              o_ref, carry_ref, state_ref, *, L, nc, t_real):
    z = z_ref[bi]
    row1 = lax.broadcasted_iota(I32, (L, 1), 0)
    zp = jnp.where(row1 == 0, carry_ref[bi], pltpu.roll(z, 1, axis=0))
    carry_ref[bi] = z[L - 1:L, :]
    zs = z + (zp - z) * mu_ref[...]

    r = zs[:, 0:C_A]
    k = zs[:, C_A:2 * C_A]
    v = zs[:, 2 * C_A:3 * C_A]
    o1 = 3 * C_A
    w_lo = zs[:, o1:o1 + D_DECAY_LORA]
    a_lo = zs[:, o1 + D_DECAY_LORA:o1 + D_DECAY_LORA + D_AAA_LORA]
    g_lo = zs[:, o1 + D_DECAY_LORA + D_AAA_LORA:C_RWKV]

    w0, a0 = vec_ref[0:1, :], vec_ref[1:2, :]
    k_k, k_a, r_k = vec_ref[2:3, :], vec_ref[3:4, :], vec_ref[4:5, :]
    lnx_g, lnx_b = vec_ref[5:6, :], vec_ref[6:7, :]
    bd = bd_ref[...]

    xw = w0 + _mm(jnp.tanh(w_lo), w2_ref[...])
    w_log = -(jnp.maximum(-xw, 0.0) + jnp.log(1.0 + jnp.exp(-jnp.abs(xw)))) - 0.5
    logw = -jnp.exp(w_log)
    a = _sigmoid(a0 + _mm(a_lo, a2_ref[...]))
    g = _mm(_sigmoid(g_lo), g2_ref[...])
    kk = k * k_k
    ss = _mm_exact_rhs(kk * kk, bd, 2)
    kk = kk / jnp.maximum(jnp.sqrt(ss), 1e-12)
    k2 = k * (1.0 + (a - 1.0) * k_a)
    if nc * L != t_real:
        valid = (c * L + row1) < t_real
        logw = jnp.where(valid, logw, 0.0)
        kk = jnp.where(valid, kk, 0.0)
        k2 = jnp.where(valid, k2, 0.0)
        v = jnp.where(valid, v, 0.0)

    row = lax.broadcasted_iota(I32, (L, L), 0)
    col = lax.broadcasted_iota(I32, (L, L), 1)
    cs = _mm_exact_lhs(jnp.where(col <= row, 1.0, 0.0).astype(BF16), logw, 3)
    cs_l = cs[L - 1:L, :]
    e_in = jnp.exp(cs)
    e_ex = jnp.exp(cs - logw)
    e_neg = jnp.exp(-cs)
    e_end = jnp.exp(cs_l - cs)
    w_l = e_in[L - 1:L, :]
    kka = kk * a
    at = -kk * e_ex
    rt = r * e_in
    bt = kka * e_neg
    kt = k2 * e_neg
    bh = kka * e_end
    kh = k2 * e_end

    rk_sum = _mm_exact_rhs(r * k2 * r_k, bd, 2)
    bonus = rk_sum * v

    gw = RWKV_GROUP * D_HEAD
    gl = RWKV_GROUP * L
    t_i = lax.broadcasted_iota(I32, (L, gl), 0)
    j_i = lax.broadcasted_iota(I32, (L, gl), 1) % L
    tril_incl = j_i <= t_i
    tril_strict = j_i < t_i
    eye_cat = jnp.where(j_i == t_i, 1.0, 0.0).astype(F32)
    same_blk = (t_i // 16) == (j_i // 16)
    mch = mch_ref[...]
    mll = mll_ref[...]
    bdf = bd[:gw, :gw].astype(F32)
    diag = lax.broadcasted_iota(I32, (gw, gw), 0) == lax.broadcasted_iota(I32, (gw, gw), 1)
    pp = RWKV_PASSES

    ys = []
    for gi in range(H_A // RWKV_GROUP):
        sl = slice(gi * gw, (gi + 1) * gw)
        at_g, rt_g, bt_g, kt_g, bh_g, kh_g, v_g = at[:, sl], rt[:, sl], bt[:, sl], kt[:, sl], bh[:, sl], kh[:, sl], v[:, sl]
        ar = jnp.concatenate([at_g, rt_g], axis=0).astype(BF16)
        g_b = _dot(ar, _bd(bt_g, mch), NT)
        g_k = _dot(ar, _bd(kt_g, mch), NT)
        a_ab = jnp.where(tril_strict, g_b[:L], 0.0)
        a_rb = jnp.where(tril_incl, g_b[L:], 0.0)
        a_ak = jnp.where(tril_strict, g_k[:L], 0.0)
        a_rk = jnp.where(tril_incl, g_k[L:], 0.0)
        tinv = _tri_inv_cat(a_ab, L, eye_cat, same_blk, mll, pp['tinv'])
        v_bd = _bd(v_g, mch)
        av = _dot(a_ak.astype(BF16), v_bd)
        p_m = _mm_bd(tinv, at_g, mch, pp['pu'])
        u0 = _mm_bd(tinv, av, mch, pp['pu'])
        y0 = _dot(a_rk.astype(BF16), v_bd) + _mm_bd(a_rb, u0, mch, 1)
        q_m = rt_g + _mm_bd(a_rb, p_m, mch, 1)
        s_bd = state_ref[bi, gi]
        ys.append(y0 + _mm(q_m, s_bd, NN, pp['out']))
        m_t = _mm(bh_g, p_m, TN, pp['state']) * bdf + jnp.where(diag, w_l[:, sl], 0.0)
        c_t = _mm(jnp.concatenate([bh_g, kh_g], axis=0), jnp.concatenate([u0, v_g], axis=0), TN, pp['state']) * bdf
        state_ref[bi, gi] = _mm(m_t, s_bd, NN, pp['state']) + c_t
    y = jnp.concatenate(ys, axis=1)

    mean = _mm_exact_rhs(y, bd, 2) * (1.0 / D_HEAD)
    dlt = y - mean
    var = _mm_exact_rhs(dlt * dlt, bd, 2) * (1.0 / D_HEAD)
    yn = dlt * lax.rsqrt(var + GN_EPS)
    o_ref[bi] = (yn * lnx_g + lnx_b + bonus) * g


def _rwkv_kernel(z_ref, shift0_ref, sbd0_ref, mu_ref, vec_ref, w2_ref, a2_ref, g2_ref, bd_ref, mch_ref, mll_ref,
                 o_ref, sbd_out_ref, shift_out_ref, carry_ref, state_ref, *, L, nc, t_real, bb):
    c = pl.program_id(1)

    @pl.when(c == 0)
    def _():
        carry_ref[...] = shift0_ref[...]
        state_ref[...] = sbd0_ref[...]

    for bi in range(bb):
        _rwkv_seq(bi, c, z_ref, mu_ref, vec_ref, w2_ref, a2_ref, g2_ref, bd_ref, mch_ref, mll_ref,
                  o_ref, carry_ref, state_ref, L=L, nc=nc, t_real=t_real)

    @pl.when(c == nc - 1)
    def _():
        sbd_out_ref[...] = state_ref[...]
        rl = t_real - 1 - (nc - 1) * L
        shift_out_ref[...] = z_ref[:, rl:rl + 1, :]


def _rwkv(z_rows, shift0, wkv0, W, bsz, t_real, L):
    nc = -(-t_real // L)
    pad = lambda a: jnp.pad(a, ((0, 0), (0, ZA_W - a.shape[-1])))
    mu = pad(W['rwkv_mu'].reshape(1, C_RWKV))
    vec = jnp.concatenate([W['rwkv_w0'][None], W['rwkv_a0'][None], W['rwkv_k_k'][None], W['rwkv_k_a'][None],
                           W['rwkv_r_k'].reshape(1, C_A), W['rwkv_lnx_g'][None], W['rwkv_lnx_b'][None],
                           jnp.zeros((1, C_A), F32)], axis=0)
    shift0p = pad(shift0).reshape(bsz, 1, ZA_W)
    hg, ng = RWKV_GROUP, H_A // RWKV_GROUP
    gw, gl = hg * D_HEAD, hg * L
    eye_h = jnp.eye(hg, dtype=F32)
    st = jnp.swapaxes(wkv0, -1, -2).reshape(bsz, ng, hg, D_HEAD, D_HEAD)
    sbd0 = (st[:, :, :, :, None, :] * eye_h[None, None, :, None, :, None]).reshape(bsz, ng, gw, gw)
    rblk = np.arange(gl) // L
    mch = jnp.asarray((rblk[:, None] == (np.arange(gw) // D_HEAD)[None, :]).astype(np.float32), dtype=BF16)
    mll = jnp.asarray((rblk[:, None] == rblk[None, :]).astype(np.float32), dtype=BF16)
    bb = RWKV_SEQS_PER_STEP if bsz % RWKV_SEQS_PER_STEP == 0 else 1
    kern = functools.partial(_rwkv_kernel, L=L, nc=nc, t_real=t_real, bb=bb)
    full = lambda shape: pl.BlockSpec(shape, lambda b, c: (0,) * len(shape))
    o_a, sbd_new, shift_new = pl.pallas_call(
        kern,
        grid=(bsz // bb, nc),
        in_specs=[pl.BlockSpec((bb, L, ZA_W), lambda b, c: (b, c, 0)),
                  pl.BlockSpec((bb, 1, ZA_W), lambda b, c: (b, 0, 0)),
                  pl.BlockSpec((bb, ng, gw, gw), lambda b, c: (b, 0, 0, 0)),
                  full((1, ZA_W)), full((8, C_A)), full((D_DECAY_LORA, C_A)), full((D_AAA_LORA, C_A)),
                  full((D_GATE_LORA, C_A)), full((C_A, C_A)), full((gl, gw)), full((gl, gl))],
        out_specs=[pl.BlockSpec((bb, L, C_A), lambda b, c: (b, c, 0)),
                   pl.BlockSpec((bb, ng, gw, gw), lambda b, c: (b, 0, 0, 0)),
                   pl.BlockSpec((bb, 1, ZA_W), lambda b, c: (b, 0, 0))],
        out_shape=[jax.ShapeDtypeStruct((bsz, nc * L, C_A), F32),
                   jax.ShapeDtypeStruct((bsz, ng, gw, gw), F32),
                   jax.ShapeDtypeStruct((bsz, 1, ZA_W), F32)],
        scratch_shapes=[pltpu.VMEM((bb, 1, ZA_W), F32), pltpu.VMEM((bb, ng, gw, gw), F32)],
        compiler_params=_cp(("parallel", "arbitrary"), VMEM_LIMIT),
        name="rwkv",
    )(z_rows.reshape(bsz, nc * L, z_rows.shape[-1]), shift0p, sbd0, mu, vec, W['rwkv_w2'].astype(BF16),
      W['rwkv_a2'].astype(BF16), W['rwkv_g2'].astype(BF16), _block_indicator(C_A, D_HEAD), mch, mll)
    wkv_new = jnp.einsum('bghchv->bghvc', sbd_new.reshape(bsz, ng, hg, D_HEAD, hg, D_HEAD))
    wkv_new = wkv_new.reshape(bsz, H_A, D_HEAD, D_HEAD)
    return o_a.reshape(bsz * nc * L, C_A), wkv_new, shift_new[:, 0, :C_RWKV]


def _qknorm_kernel(q_ref, k_ref, gq_ref, gk_ref, bd_ref, qn_ref, kn_ref):
    bd = bd_ref[...]
    q = q_ref[...]
    k = k_ref[...]
    msq = _mm_exact_rhs(q * q, bd, 2) * (1.0 / D_HEAD)
    msk = _mm_exact_rhs(k * k, bd, 2) * (1.0 / D_HEAD)
    qn_ref[...] = q * lax.rsqrt(msq + RMS_EPS) * gq_ref[...] * (D_HEAD ** -0.5)
    kn_ref[...] = k * lax.rsqrt(msk + RMS_EPS) * gk_ref[...]


def _qknorm(z, gq, gk):
    n = z.shape[0]
    tm = min(512, n)
    tile = lambda g: jnp.tile(g.reshape(1, D_HEAD), (1, C_B // D_HEAD))
    return pl.pallas_call(
        _qknorm_kernel,
        grid=(n // tm,),
        in_specs=[pl.BlockSpec((tm, C_B), lambda i: (i, QB0 // C_B)),
                  pl.BlockSpec((tm, C_B), lambda i: (i, KB0 // C_B)),
                  pl.BlockSpec((1, C_B), lambda i: (0, 0)),
                  pl.BlockSpec((1, C_B), lambda i: (0, 0)),
                  pl.BlockSpec((C_B, C_B), lambda i: (0, 0))],
        out_specs=[pl.BlockSpec((tm, C_B), lambda i: (i, 0)), pl.BlockSpec((tm, C_B), lambda i: (i, 0))],
        out_shape=[jax.ShapeDtypeStruct((n, C_B), F32), jax.ShapeDtypeStruct((n, C_B), F32)],
        compiler_params=_cp(("parallel",)),
        name="qknorm",
    )(z, z, tile(gq), tile(gk), _block_indicator(C_B, D_HEAD))


def _lam_value(lam_ref, lam_init):
    lv = lam_ref[...]
    l1 = jnp.exp(jnp.sum(lv[0:1, :] * lv[1:2, :], axis=-1, keepdims=True))
    l2 = jnp.exp(jnp.sum(lv[2:3, :] * lv[3:4, :], axis=-1, keepdims=True))
    return l1 - l2 + lam_init


def _diff_attn_kernel(qi_tab, kj_tab, q_ref, k_ref, v_ref, lam_ref, g_ref, o_ref,
                      qs_ref, m_ref, l_ref, acc_ref, *, tq, tk, rg, lam_init):
    p = pl.program_id(2)
    qi = qi_tab[p]
    kj = kj_tab[p]

    @pl.when(kj == 0)
    def _():
        q = q_ref[...]
        lane = lax.broadcasted_iota(I32, q.shape, 1)
        qs_ref[0:tq, :] = jnp.where(lane < D_HEAD, q, 0.0).astype(BF16)
        qs_ref[tq:2 * tq, :] = jnp.where(lane >= D_HEAD, q, 0.0).astype(BF16)
        m_ref[...] = jnp.full(m_ref.shape, NEG_BIG, F32)
        l_ref[...] = jnp.zeros(l_ref.shape, F32)
        acc_ref[...] = jnp.zeros(acc_ref.shape, F32)

    def body(masked):
        k = k_ref[...].astype(BF16)
        v = v_ref[...].astype(BF16)
        for g in range(2 * tq // rg):
            cs = pl.ds(g * rg, rg)
            s = _dot(k, qs_ref[cs, :], NT)
            if masked:
                kpos = kj * tk + lax.broadcasted_iota(I32, (tk, rg), 0)
                qpos = qi * tq + (g * rg) % tq + lax.broadcasted_iota(I32, (tk, rg), 1)
                s = jnp.where(kpos <= qpos, s, NEG_BIG)
            m_prev = m_ref[:, cs]
            m_new = jnp.maximum(m_prev, jnp.max(s, axis=0, keepdims=True))
            alpha = jnp.exp(m_prev - m_new)
            pr = jnp.exp(s - m_new)
            l_ref[:, cs] = alpha * l_ref[:, cs] + jnp.sum(pr, axis=0, keepdims=True)
            acc_ref[:, cs] = alpha * acc_ref[:, cs] + _dot(v, pr.astype(BF16), TN)
            m_ref[:, cs] = m_new

    crosses = (kj + 1) * tk - 1 > qi * tq
    pl.when(crosses)(lambda: body(True))
    pl.when(jnp.logical_not(crosses))(lambda: body(False))

    @pl.when(kj == ((qi + 1) * tq - 1) // tk)
    def _():
        lam = _lam_value(lam_ref, lam_init)
        acc = acc_ref[...]
        inv_l = 1.0 / l_ref[...]
        o = acc[:, :tq] * inv_l[:, :tq] - lam * (acc[:, tq:] * inv_l[:, tq:])
        ms = jnp.mean(o * o, axis=0, keepdims=True)
        o = o * lax.rsqrt(ms + RMS_EPS) * g_ref[...] * (1.0 - lam_init)
        o_ref[...] = o.T


DIFF_ROW_GROUP = 256


def _causal_pairs(nq, tq, tk, descending):
    qi_l, kj_l = [], []
    for qi in range(nq):
        last = ((qi + 1) * tq - 1) // tk
        ks = range(last, -1, -1) if descending else range(last + 1)
        for kj in ks:
            qi_l.append(qi)
            kj_l.append(kj)
    return jnp.asarray(qi_l, I32), jnp.asarray(kj_l, I32)


def _diff_attn(qn, kn, z, lam_w, subln_g, bsz, t_len, lam_init):
    tq = tk = min(512, t_len)
    nq, nk = t_len // tq, t_len // tk
    qi_tab, kj_tab = _causal_pairs(nq, tq, tk, False)
    vcol = VB0 // LANES
    kern = functools.partial(_diff_attn_kernel, tq=tq, tk=tk, rg=min(DIFF_ROW_GROUP, tq), lam_init=lam_init)
    grid_spec = pltpu.PrefetchScalarGridSpec(
        num_scalar_prefetch=2,
        grid=(bsz, H_B, int(qi_tab.shape[0])),
        in_specs=[pl.BlockSpec((tq, LANES), lambda b, h, p, qt, kt: (b * nq + qt[p], h)),
                  pl.BlockSpec((tk, LANES), lambda b, h, p, qt, kt: (b * nk + kt[p], h)),
                  pl.BlockSpec((tk, LANES), lambda b, h, p, qt, kt: (b * nk + kt[p], vcol + h)),
                  pl.BlockSpec((4, D_HEAD), lambda b, h, p, qt, kt: (0, 0)),
                  pl.BlockSpec((LANES, 1), lambda b, h, p, qt, kt: (0, 0))],
        out_specs=pl.BlockSpec((tq, LANES), lambda b, h, p, qt, kt: (b * nq + qt[p], h)),
        scratch_shapes=[pltpu.VMEM((2 * tq, LANES), BF16), pltpu.VMEM((1, 2 * tq), F32),
                        pltpu.VMEM((1, 2 * tq), F32), pltpu.VMEM((LANES, 2 * tq), F32)])
    return pl.pallas_call(
        kern, grid_spec=grid_spec,
        out_shape=jax.ShapeDtypeStruct((bsz * t_len, C_B), F32),
        compiler_params=_cp(("parallel", "parallel", "arbitrary")),
        name="diff_attn",
    )(qi_tab, kj_tab, qn, kn, z, lam_w, subln_g.reshape(2 * D_HEAD, 1))


def _sb_attn_kernel(qi_tab, kj_tab, q_ref, k_ref, v_ref, o_ref, qs_ref, c_ref, acc_ref, done_ref,
                    *, tq, tkb, sub):
    p = pl.program_id(2)
    qi = qi_tab[p]
    kj = kj_tab[p]
    first_kj = ((qi + 1) * tq - 1) // tkb

    @pl.when(kj == first_kj)
    def _():
        q = q_ref[...] * (D_HEAD ** -0.5)
        lane = lax.broadcasted_iota(I32, q.shape, 1)
        qs_ref[0:tq, :] = jnp.where(lane < D_HEAD, q, 0.0).astype(BF16)
        qs_ref[tq:2 * tq, :] = jnp.where(lane >= D_HEAD, q, 0.0).astype(BF16)
        c_ref[...] = jnp.zeros(c_ref.shape, F32)
        acc_ref[...] = jnp.zeros(acc_ref.shape, F32)
        done_ref[0] = 0

    rr = lax.broadcasted_iota(I32, (sub, sub), 0)
    cc = lax.broadcasted_iota(I32, (sub, sub), 1)
    upper = jnp.where(rr > cc, 1.0, 0.0).astype(BF16)

    def body(masked):
        qs = qs_ref[...]
        for sb in reversed(range(tkb // sub)):
            ksl = slice(sb * sub, (sb + 1) * sub)
            z = _dot(qs, k_ref[ksl, :].astype(BF16), NT)
            logb, l1 = _log_sigmoid_pair(z)
            if masked:
                qpos = qi * tq + lax.broadcasted_iota(I32, (tq, sub), 0)
                kpos = kj * tkb + sb * sub + lax.broadcasted_iota(I32, (tq, sub), 1)
                ok = kpos < qpos
                ok = jnp.concatenate([ok, ok], axis=0)
                l1 = jnp.where(ok, l1, 0.0)
            suf = _mm_exact_rhs(l1, upper, 2)
            c_prev = c_ref[...]
            att = jnp.exp(logb + suf + c_prev)
            if masked:
                att = jnp.where(ok, att, 0.0)
            acc_ref[...] += _dot(att.astype(BF16), v_ref[ksl, :].astype(BF16))
            c_ref[...] = c_prev + jnp.sum(l1, axis=-1, keepdims=True)
        done_ref[0] = (jnp.max(c_ref[...]) < SB_DEAD).astype(I32)

    live = done_ref[0] == 0
    crosses = (kj + 1) * tkb > qi * tq
    pl.when(jnp.logical_and(live, crosses))(lambda: body(True))
    pl.when(jnp.logical_and(live, jnp.logical_not(crosses)))(lambda: body(False))

    @pl.when(kj == 0)
    def _():
        acc = acc_ref[...]
        lane = lax.broadcasted_iota(I32, (tq, LANES), 1)
        o_ref[...] = jnp.where(lane < D_HEAD, acc[:tq], acc[tq:])


def _sb_attn(z, bsz, t_len):
    tq = tkb = min(512, t_len)
    sub = min(128, tkb)
    nq, nk = t_len // tq, t_len // tkb
    qi_tab, kj_tab = _causal_pairs(nq, tq, tkb, True)
    qc, kc, vc = QC0 // LANES, KC0 // LANES, VC0 // LANES
    kern = functools.partial(_sb_attn_kernel, tq=tq, tkb=tkb, sub=sub)
    grid_spec = pltpu.PrefetchScalarGridSpec(
        num_scalar_prefetch=2,
        grid=(bsz, H_C // 2, int(qi_tab.shape[0])),
        in_specs=[pl.BlockSpec((tq, LANES), lambda b, h, p, qt, kt: (b * nq + qt[p], qc + h)),
                  pl.BlockSpec((tkb, LANES), lambda b, h, p, qt, kt: (b * nk + kt[p], kc + h)),
                  pl.BlockSpec((tkb, LANES), lambda b, h, p, qt, kt: (b * nk + kt[p], vc + h))],
        out_specs=pl.BlockSpec((tq, LANES), lambda b, h, p, qt, kt: (b * nq + qt[p], h)),
        scratch_shapes=[pltpu.VMEM((2 * tq, LANES), BF16), pltpu.VMEM((2 * tq, 1), F32),
                        pltpu.VMEM((2 * tq, LANES), F32), pltpu.SMEM((1,), I32)])
    return pl.pallas_call(
        kern, grid_spec=grid_spec,
        out_shape=jax.ShapeDtypeStruct((bsz * t_len, C_C), F32),
        compiler_params=_cp(("parallel", "parallel", "arbitrary")),
        name="sb_attn",
    )(qi_tab, kj_tab, z, z, z)


DEC_ROWS = 16


def _diff_dec_kernel(pt_ref, q_ref, ks_ref, vs_ref, lam_ref, g_ref, *rest, G, lam_init):
    k_refs, v_refs = rest[:G], rest[G:2 * G]
    o_ref, qm_ref, m_ref, l_ref, acc_ref = rest[2 * G:]
    j = pl.program_id(1)
    hd = 2 * D_HEAD
    rowi = lax.broadcasted_iota(I32, (DEC_ROWS, hd), 0)
    lane = lax.broadcasted_iota(I32, (DEC_ROWS, hd), 1)
    sel = (lane // D_HEAD) == rowi

    def head_q(h):
        return jnp.where(sel, q_ref[0][:, h * hd:(h + 1) * hd], 0.0)

    @pl.when(j == 0)
    def _():
        for h in range(H_B):
            qm_ref[h * DEC_ROWS:(h + 1) * DEC_ROWS, :] = head_q(h).astype(BF16)
        m_ref[...] = jnp.full(m_ref.shape, NEG_BIG, F32)
        l_ref[...] = jnp.zeros(l_ref.shape, F32)
        acc_ref[...] = jnp.zeros(acc_ref.shape, F32)

    qm = qm_ref[...]
    head_page = lambda ref, h: ref[pl.ds(h, PAGE_SIZE, stride=H_B), :].astype(BF16)
    s = jnp.concatenate(
        [jnp.concatenate([_dot(qm[h * DEC_ROWS:(h + 1) * DEC_ROWS], head_page(k_refs[gi], h), NT)
                          for gi in range(G)], axis=1) for h in range(H_B)], axis=0)
    m_prev = m_ref[...]
    m_new = jnp.maximum(m_prev, jnp.max(s, axis=-1, keepdims=True))
    alpha = jnp.exp(m_prev - m_new)
    pr = jnp.exp(s - m_new)
    l_ref[...] = alpha * l_ref[...] + jnp.sum(pr, axis=-1, keepdims=True)
    pr = pr.astype(BF16)
    pvs = []
    for h in range(H_B):
        pv = jnp.zeros((DEC_ROWS, hd), F32)
        for gi in range(G):
            pv = pv + _dot(pr[h * DEC_ROWS:(h + 1) * DEC_ROWS, gi * PAGE_SIZE:(gi + 1) * PAGE_SIZE],
                           head_page(v_refs[gi], h))
        pvs.append(pv)
    acc_ref[...] = alpha * acc_ref[...] + jnp.concatenate(pvs, axis=0)
    m_ref[...] = m_new

    @pl.when(j == pl.num_programs(1) - 1)
    def _():
        lam = _lam_value(lam_ref, lam_init)
        outs = []
        for h in range(H_B):
            rs = slice(h * DEC_ROWS, (h + 1) * DEC_ROWS)
            cs = slice(h * hd, (h + 1) * hd)
            s = jnp.sum(head_q(h) * ks_ref[0][:, cs], axis=-1, keepdims=True)
            m_prev = m_ref[rs, :]
            m_new = jnp.maximum(m_prev, s)
            alpha = jnp.exp(m_prev - m_new)
            pr = jnp.exp(s - m_new)
            l_fin = alpha * l_ref[rs, :] + pr
            att = (alpha * acc_ref[rs, :] + pr * vs_ref[0][:, cs]) / l_fin
            o = att[0:1, :] - lam * att[1:2, :]
            ms = jnp.mean(o * o, axis=-1, keepdims=True)
            outs.append(o * lax.rsqrt(ms + RMS_EPS))
        o_ref[0] = jnp.concatenate(outs, axis=-1) * g_ref[...] * (1.0 - lam_init)


def _pages_per_step(n_pages, g):
    while n_pages % g:
        g //= 2
    return g


def _diff_dec(qn, kn, vb, cache_k, cache_v, page_table, layer, lam_w, subln_g, lam_init):
    bsz, n_pages = page_table.shape
    G = _pages_per_step(n_pages, 16)
    nl = cache_k.shape[1]
    hd = 2 * D_HEAD
    ck = cache_k.reshape(cache_k.shape[0], nl, PAGE_SIZE * H_B, hd)
    cv = cache_v.reshape(cache_v.shape[0], nl, PAGE_SIZE * H_B, hd)
    row3 = lambda a: a.reshape(bsz, 1, C_B)
    vec_spec = pl.BlockSpec((1, 1, C_B), lambda b, j, pt: (b, 0, 0))
    page_spec = lambda gi: pl.BlockSpec((None, None, PAGE_SIZE * H_B, hd),
                                        lambda b, j, pt: (pt[b, j * G + gi], layer, 0, 0))
    kern = functools.partial(_diff_dec_kernel, G=G, lam_init=lam_init)
    grid_spec = pltpu.PrefetchScalarGridSpec(
        num_scalar_prefetch=1,
        grid=(bsz, n_pages // G),
        in_specs=[vec_spec, vec_spec, vec_spec,
                  pl.BlockSpec((4, D_HEAD), lambda b, j, pt: (0, 0)),
                  pl.BlockSpec((1, C_B), lambda b, j, pt: (0, 0))]
                 + [page_spec(gi) for gi in range(G)] + [page_spec(gi) for gi in range(G)],
        out_specs=pl.BlockSpec((1, 1, C_B), lambda b, j, pt: (b, 0, 0)),
        scratch_shapes=[pltpu.VMEM((H_B * DEC_ROWS, hd), BF16), pltpu.VMEM((H_B * DEC_ROWS, 1), F32),
                        pltpu.VMEM((H_B * DEC_ROWS, 1), F32), pltpu.VMEM((H_B * DEC_ROWS, hd), F32)])
    out = pl.pallas_call(
        kern, grid_spec=grid_spec,
        out_shape=jax.ShapeDtypeStruct((bsz, 1, C_B), F32),
        compiler_params=_cp(("parallel", "arbitrary")),
        name="diff_dec",
    )(page_table, row3(qn), row3(kn), row3(vb), lam_w,
      jnp.tile(subln_g.reshape(1, 2 * D_HEAD), (1, H_B)), *([ck] * G), *([cv] * G))
    return out.reshape(bsz, C_B)


def _sb_dec_kernel(pt_ref, dn_ref, q_ref, acc0_ref, c0_ref, *rest, G):
    k_refs, v_refs = rest[:G], rest[G:2 * G]
    o_ref, acc_out_ref, c_out_ref, qm_ref, c_ref, acc_ref, done_ref = rest[2 * G:]
    b = pl.program_id(0)
    j = pl.program_id(1)
    shape = (DEC_ROWS, C_C)
    rowi = lax.broadcasted_iota(I32, shape, 0)
    lane = lax.broadcasted_iota(I32, shape, 1)
    sel = (lane // D_HEAD) == rowi

    @pl.when(j == 0)
    def _():
        qm_ref[...] = jnp.where(sel, q_ref[0] * (D_HEAD ** -0.5), 0.0).astype(BF16)
        c_ref[...] = c0_ref[0][:, 0:1]
        acc_ref[...] = acc0_ref[0]
        done_ref[0] = dn_ref[b]

    @pl.when(done_ref[0] == 0)
    def _():
        rr = lax.broadcasted_iota(I32, (PAGE_SIZE, PAGE_SIZE), 0)
        cc = lax.broadcasted_iota(I32, (PAGE_SIZE, PAGE_SIZE), 1)
        upper = jnp.where(rr > cc, 1.0, 0.0).astype(BF16)
        qm = qm_ref[...]
        c_run = c_ref[...]
        acc = acc_ref[...]
        for gi in range(G):
            z = _dot(qm, k_refs[gi][...].astype(BF16))
            logb, l1 = _log_sigmoid_pair(z)
            suf = _mm_exact_rhs(l1, upper, 2)
            att = jnp.exp(logb + suf + c_run)
            acc = acc + _dot(att.astype(BF16), v_refs[gi][...].astype(BF16), NT)
            c_run = c_run + jnp.sum(l1, axis=-1, keepdims=True)
        acc_ref[...] = acc
        c_ref[...] = c_run
        row1 = lax.broadcasted_iota(I32, (DEC_ROWS, 1), 0)
        cmax = jnp.max(jnp.where(row1 < H_C, c_run, NEG_BIG))
        done_ref[0] = (cmax < SB_DEAD).astype(I32)

    @pl.when(j == pl.num_programs(1) - 1)
    def _():
        acc = acc_ref[...]
        o_ref[0] = jnp.sum(jnp.where(sel, acc, 0.0), axis=0, keepdims=True)
        acc_out_ref[0] = acc
        c_out_ref[0] = jnp.broadcast_to(c_ref[...], (DEC_ROWS, LANES))


def _sb_dec_phase(q3, ck, cv, page_table, done, acc0, c0, layer, start, n_phase, G):
    bsz, n_pages = page_table.shape

    def page_spec(gi):
        def idx(b, j, pt, dn):
            page = pt[b, n_pages - 1 - (start + j * G + gi)]
            return (jnp.where(dn[b] == 0, page, pt[0, 0]), layer, 0, 0)
        return pl.BlockSpec((None, None, C_C, PAGE_SIZE), idx)

    row = lambda shape: pl.BlockSpec((1,) + shape, lambda b, j, pt, dn: (b, 0, 0))
    kern = functools.partial(_sb_dec_kernel, G=G)
    grid_spec = pltpu.PrefetchScalarGridSpec(
        num_scalar_prefetch=2,
        grid=(bsz, n_phase // G),
        in_specs=[row((1, C_C)), row((DEC_ROWS, C_C)), row((DEC_ROWS, LANES))]
                 + [page_spec(gi) for gi in range(G)] + [page_spec(gi) for gi in range(G)],
        out_specs=[row((1, C_C)), row((DEC_ROWS, C_C)), row((DEC_ROWS, LANES))],
        scratch_shapes=[pltpu.VMEM((DEC_ROWS, C_C), BF16), pltpu.VMEM((DEC_ROWS, 1), F32),
                        pltpu.VMEM((DEC_ROWS, C_C), F32), pltpu.SMEM((1,), I32)])
    return pl.pallas_call(
        kern, grid_spec=grid_spec,
        out_shape=[jax.ShapeDtypeStruct((bsz, 1, C_C), F32), jax.ShapeDtypeStruct((bsz, DEC_ROWS, C_C), F32),
                   jax.ShapeDtypeStruct((bsz, DEC_ROWS, LANES), F32)],
        compiler_params=_cp(("parallel", "arbitrary")),
        name="sb_dec",
    )(page_table, done, q3, acc0, c0, *([ck] * G), *([cv] * G))


SB_DEC_HEAD_PAGES = 8
SB_DEC_TAIL_GROUP = 12


def _sb_dec(qc, cache_k, cache_v, page_table, layer):
    bsz, n_pages = page_table.shape
    nl = cache_k.shape[1]
    ck = jnp.transpose(cache_k, (0, 1, 3, 4, 2)).reshape(cache_k.shape[0], nl, C_C, PAGE_SIZE)
    cv = jnp.transpose(cache_v, (0, 1, 3, 4, 2)).reshape(cache_v.shape[0], nl, C_C, PAGE_SIZE)
    q3 = qc.reshape(bsz, 1, C_C)
    n_head = min(SB_DEC_HEAD_PAGES, n_pages)
    acc0 = jnp.zeros((bsz, DEC_ROWS, C_C), F32)
    c0 = jnp.zeros((bsz, DEC_ROWS, LANES), F32)
    live = jnp.zeros((bsz,), I32)
    out, acc, c = _sb_dec_phase(q3, ck, cv, page_table, live, acc0, c0, layer, 0, n_head, n_head)
    n_tail = n_pages - n_head
    if n_tail:
        done = (jnp.max(c[:, :H_C, 0], axis=1) < SB_DEAD).astype(I32)
        out, _, _ = _sb_dec_phase(q3, ck, cv, page_table, done, acc, c, layer, n_head, n_tail,
                                  _pages_per_step(n_tail, SB_DEC_TAIL_GROUP))
    return out.reshape(bsz, C_C)


def _merge_kernel(h_ref, oa_ref, ob_ref, oc_ref, g0_ref, g1_ref, g2_ref, wa_ref, wb_ref, wc_ref, wo_ref,
                  gn_ref, rw_ref, rb_ref, h_out_ref, xn_ref, ridx_ref, rgate_ref):
    mix = (_sigmoid(g0_ref[...]) * _dot(oa_ref[...].astype(BF16), wa_ref[...])
           + _sigmoid(g1_ref[...]) * _dot(ob_ref[...].astype(BF16), wb_ref[...])
           + _sigmoid(g2_ref[...]) * _dot(oc_ref[...].astype(BF16), wc_ref[...]))
    h = h_ref[...] + _dot(mix.astype(BF16), wo_ref[...])
    h_out_ref[...] = h
    ms = jnp.mean(h * h, axis=-1, keepdims=True)
    xn = h * lax.rsqrt(ms + RMS_EPS) * gn_ref[...]
    xn_ref[...] = xn
    logits = _mm(xn, rw_ref[...], NN, 3) + rb_ref[...]
    tm = logits.shape[0]
    lane = lax.broadcasted_iota(I32, logits.shape, 1)
    lane_o = lax.broadcasted_iota(I32, (tm, LANES), 1)
    work = logits
    vals, idxs = [], []
    for _ in range(TOP_K):
        mx = jnp.max(work, axis=-1, keepdims=True)
        ik = jnp.min(jnp.where(work == mx, lane, N_EXPERTS), axis=-1, keepdims=True)
        vals.append(mx)
        idxs.append(ik)
        work = jnp.where(lane == ik, -jnp.inf, work)
    es = [jnp.exp(vk - vals[0]) for vk in vals]
    den = es[0] + es[1] + es[2] + es[3]
    ridx = jnp.zeros((tm, LANES), I32)
    rgate = jnp.zeros((tm, LANES), F32)
    for kx in range(TOP_K):
        ridx = jnp.where(lane_o == kx, idxs[kx], ridx)
        rgate = jnp.where(lane_o == kx, es[kx] / den, rgate)
    ridx_ref[...] = ridx
    rgate_ref[...] = rgate


def _merge(h, o_a, o_b, o_c, z, W):
    n, d = h.shape
    tm = min(512, n)
    gcol = ZG0 // d
    row = lambda c: pl.BlockSpec((tm, c), lambda i: (i, 0))
    full = lambda a: pl.BlockSpec(a.shape, lambda i: (0,) * a.ndim)
    wa, wb, wc, wo = (W['w_branch_a'].astype(BF16), W['w_branch_b'].astype(BF16),
                      W['w_branch_c'].astype(BF16), W['w_out'].astype(BF16))
    gn = W['norm_ffn_g'].reshape(1, d)
    rw = W['router_w']
    rb = W['router_b'].reshape(1, N_EXPERTS)
    return pl.pallas_call(
        _merge_kernel,
        grid=(n // tm,),
        in_specs=[row(d), row(C_A), row(C_B), row(C_C),
                  pl.BlockSpec((tm, d), lambda i: (i, gcol)),
                  pl.BlockSpec((tm, d), lambda i: (i, gcol + 1)),
                  pl.BlockSpec((tm, d), lambda i: (i, gcol + 2)),
                  full(wa), full(wb), full(wc), full(wo), full(gn), full(rw), full(rb)],
        out_specs=[row(d), row(d), row(LANES), row(LANES)],
        out_shape=[jax.ShapeDtypeStruct((n, d), F32), jax.ShapeDtypeStruct((n, d), F32),
                   jax.ShapeDtypeStruct((n, LANES), I32), jax.ShapeDtypeStruct((n, LANES), F32)],
        compiler_params=_cp(("parallel",), VMEM_LIMIT),
        name="merge",
    )(h, o_a, o_b, o_c, z, z, z, wa, wb, wc, wo, gn, rw, rb)


def _route_kernel(ridx_ref, rpos_ref, cnt_ref, base_ref):
    i = pl.program_id(0)

    @pl.when(i == 0)
    def _():
        base_ref[...] = jnp.zeros(base_ref.shape, F32)

    ridx = ridx_ref[...]
    tm = ridx.shape[0]
    lane_e = lax.broadcasted_iota(I32, (tm, N_EXPERTS), 1)
    lane_o = lax.broadcasted_iota(I32, (tm, LANES), 1)
    onehots = [(lane_e == ridx[:, kx:kx + 1]).astype(F32) for kx in range(TOP_K)]
    sel = onehots[0] + onehots[1] + onehots[2] + onehots[3]
    rr = lax.broadcasted_iota(I32, (tm, tm), 0)
    cc = lax.broadcasted_iota(I32, (tm, tm), 1)
    before = jnp.where(cc < rr, 1.0, 0.0).astype(BF16)
    rank = _dot(before, sel.astype(BF16)) + base_ref[...]
    rpos = jnp.zeros((tm, LANES), I32)
    for kx in range(TOP_K):
        pk = jnp.sum(onehots[kx] * rank, axis=-1, keepdims=True)
        rpos = jnp.where(lane_o == kx, pk.astype(I32), rpos)
    rpos_ref[...] = rpos
    base_ref[...] = base_ref[...] + jnp.sum(sel, axis=0, keepdims=True)
    cnt_ref[...] = base_ref[...]


def _route(ridx):
    n = ridx.shape[0]
    tm = min(256, n)
    return pl.pallas_call(
        _route_kernel,
        grid=(n // tm,),
        in_specs=[pl.BlockSpec((tm, LANES), lambda i: (i, 0))],
        out_specs=[pl.BlockSpec((tm, LANES), lambda i: (i, 0)), pl.BlockSpec((1, N_EXPERTS), lambda i: (0, 0))],
        out_shape=[jax.ShapeDtypeStruct((n, LANES), I32), jax.ShapeDtypeStruct((1, N_EXPERTS), F32)],
        scratch_shapes=[pltpu.VMEM((1, N_EXPERTS), F32)],
        compiler_params=_cp(("arbitrary",)),
        name="route",
    )(ridx)


def _row_copy(src, src_row, dst, dst_row, sem):
    return pltpu.make_async_copy(src.at[pl.ds(pl.multiple_of(src_row * ROW_TILES, ROW_TILES), ROW_TILES)],
                                 dst.at[pl.ds(pl.multiple_of(dst_row * ROW_TILES, ROW_TILES), ROW_TILES)], sem)


def _dispatch_kernel(dest_ref, tail_ref, nused_ref, x_ref, xs_ref, zbuf_ref, sem_z, sem, *, tm, blk, n_blk):
    i = pl.program_id(0)

    def zero_copy(row0):
        return pltpu.make_async_copy(
            zbuf_ref, xs_ref.at[pl.ds(pl.multiple_of(row0 * ROW_TILES, ROW_TILES), blk * ROW_TILES)], sem_z)

    @pl.when(i == 0)
    def _():
        zbuf_ref[...] = jnp.zeros(zbuf_ref.shape, F32)
        for e in range(N_EXPERTS):
            pl.when(tail_ref[e] >= 0)(lambda e=e: zero_copy(tail_ref[e]).start())
        for e in range(N_EXPERTS):
            pl.when(tail_ref[e] >= 0)(lambda e=e: zero_copy(tail_ref[e]).wait())

        def fill(b, carry):
            cp = zero_copy(b * blk)
            cp.start()
            cp.wait()
            return carry

        lax.fori_loop(nused_ref[0], n_blk, fill, 0)

    def issue(r, carry):
        for kx in range(TOP_K):
            _row_copy(x_ref, r, xs_ref, dest_ref[r * TOP_K + kx], sem).start()
        return carry

    lax.fori_loop(0, tm, issue, 0)

    def drain(r, carry):
        for kx in range(TOP_K):
            _row_copy(x_ref, 0, xs_ref, 0, sem).wait()
        return carry

    lax.fori_loop(0, tm, drain, 0)


def _dispatch(xn_tiles, dest_flat, tail_rows, nused, n_blk, blk):
    n = xn_tiles.shape[0] // ROW_TILES
    tm = min(256, n)
    kern = functools.partial(_dispatch_kernel, tm=tm, blk=blk, n_blk=n_blk)
    return pl.pallas_call(
        kern,
        grid=(n // tm,),
        in_specs=[pl.BlockSpec((tm * TOP_K,), lambda i: (i,), memory_space=pltpu.SMEM),
                  pl.BlockSpec(memory_space=pltpu.SMEM),
                  pl.BlockSpec(memory_space=pltpu.SMEM),
                  pl.BlockSpec((tm * ROW_TILES, LANES), lambda i: (i, 0))],
        out_specs=pl.BlockSpec(memory_space=pl.ANY),
        out_shape=jax.ShapeDtypeStruct((n_blk * blk * ROW_TILES, LANES), F32),
        scratch_shapes=[pltpu.VMEM((blk * ROW_TILES, LANES), F32),
                        pltpu.SemaphoreType.DMA(()), pltpu.SemaphoreType.DMA(())],
        compiler_params=_cp(("arbitrary",)),
        name="moe_dispatch",
    )(dest_flat, tail_rows, nused, xn_tiles)


def _rows_from_tiles(ref, n):
    return jnp.concatenate([ref[pl.ds(s, n, stride=ROW_TILES), :] for s in range(ROW_TILES)], axis=1)


def _expert_kernel(blk_e, nused, x_ref, wgu_ref, bgu_ref, wd_ref, bd_ref, y_ref, wgu_bf, wd_bf, *, blk, d_ff):
    i = pl.program_id(0)

    @pl.when(i < nused[0])
    def _():
        e = blk_e[i]
        e_prev = blk_e[jnp.maximum(i - 1, 0)]

        @pl.when(jnp.logical_or(i == 0, e != e_prev))
        def _():
            wgu_bf[...] = wgu_ref[...].astype(BF16)
            wd_bf[...] = wd_ref[...].astype(BF16)

        x = _rows_from_tiles(x_ref, blk).astype(BF16)
        hh = _dot(x, wgu_bf[...]) + bgu_ref[...]
        hg = jnp.minimum(hh[:, :d_ff], SWIGLU_LIMIT)
        hl = jnp.clip(hh[:, d_ff:], -SWIGLU_LIMIT, SWIGLU_LIMIT)
        act = hg * _sigmoid(SWIGLU_ALPHA * hg) * (hl + 1.0)
        y = _dot(act.astype(BF16), wd_bf[...]) + bd_ref[...]
        for s in range(ROW_TILES):
            y_ref[pl.ds(s, blk, stride=ROW_TILES), :] = y[:, s * LANES:(s + 1) * LANES]

    @pl.when(i >= nused[0])
    def _():
        y_ref[...] = jnp.zeros(y_ref.shape, F32)


def _experts(xs, blk_e, nused, w_gu, b_gu, w_down, b_down, blk):
    n_blk = blk_e.shape[0]
    n_e, d, f2 = w_gu.shape
    d_ff = f2 // 2
    kern = functools.partial(_expert_kernel, blk=blk, d_ff=d_ff)
    last = lambda i, be, nu: jnp.minimum(i, nu[0] - 1)
    grid_spec = pltpu.PrefetchScalarGridSpec(
        num_scalar_prefetch=2,
        grid=(n_blk,),
        in_specs=[pl.BlockSpec((blk * ROW_TILES, LANES), lambda i, be, nu: (last(i, be, nu), 0)),
                  pl.BlockSpec((None, d, f2), lambda i, be, nu: (be[i], 0, 0)),
                  pl.BlockSpec((None, 1, f2), lambda i, be, nu: (be[i], 0, 0)),
                  pl.BlockSpec((None, d_ff, d), lambda i, be, nu: (be[i], 0, 0)),
                  pl.BlockSpec((None, 1, d), lambda i, be, nu: (be[i], 0, 0))],
        out_specs=pl.BlockSpec((blk * ROW_TILES, LANES), lambda i, be, nu: (i, 0)),
        scratch_shapes=[pltpu.VMEM((d, f2), BF16), pltpu.VMEM((d_ff, d), BF16)])
    return pl.pallas_call(
        kern, grid_spec=grid_spec,
        out_shape=jax.ShapeDtypeStruct(xs.shape, F32),
        compiler_params=_cp(("arbitrary",), VMEM_LIMIT),
        name="moe_experts",
    )(blk_e, nused, xs, w_gu, b_gu.reshape(n_e, 1, f2), w_down, b_down.reshape(n_e, 1, d))


def _combine_kernel(dest_ref, gate_ref, h_ref, pe_ref, yb_ref, gp_ref, pw_ref, pg_ref, o_ref, buf_ref, sem, *, tm):
    def issue(r, carry):
        for kx in range(TOP_K):
            _row_copy(yb_ref, dest_ref[r * TOP_K + kx], buf_ref.at[kx], r, sem).start()
        return carry

    lax.fori_loop(0, tm, issue, 0)

    def drain(r, carry):
        for kx in range(TOP_K):
            _row_copy(yb_ref, 0, buf_ref.at[kx], 0, sem).wait()
        return carry

    lax.fori_loop(0, tm, drain, 0)

    gate = gate_ref[...]
    h = h_ref[...]
    for kx in range(TOP_K):
        h = h + gate[:, kx:kx + 1] * _rows_from_tiles(buf_ref.at[kx], tm)
    ms = jnp.mean(h * h, axis=-1, keepdims=True)
    xn = (h * lax.rsqrt(ms + RMS_EPS) * gp_ref[...]).astype(BF16)
    o_ref[...] = h + _dot(pe_ref[...].astype(BF16), pw_ref[...]) * _sigmoid(_dot(xn, pg_ref[...]))


def _combine(h, pe, yb, dest_flat, rgate, W):
    n, d = h.shape
    tm = min(256, n)
    gp = W['norm_ple_g'].reshape(1, d)
    pw = W['ple_w'].astype(BF16)
    pg = W['ple_gate_w'].astype(BF16)
    full = lambda a: pl.BlockSpec(a.shape, lambda i: (0,) * a.ndim)
    kern = functools.partial(_combine_kernel, tm=tm)
    return pl.pallas_call(
        kern,
        grid=(n // tm,),
        in_specs=[pl.BlockSpec((tm * TOP_K,), lambda i: (i,), memory_space=pltpu.SMEM),
                  pl.BlockSpec((tm, LANES), lambda i: (i, 0)),
                  pl.BlockSpec((tm, d), lambda i: (i, 0)),
                  pl.BlockSpec((tm, pe.shape[1]), lambda i: (i, 0)),
                  pl.BlockSpec(memory_space=pl.ANY),
                  full(gp), full(pw), full(pg)],
        out_specs=pl.BlockSpec((tm, d), lambda i: (i, 0)),
        out_shape=jax.ShapeDtypeStruct((n, d), F32),
        scratch_shapes=[pltpu.VMEM((TOP_K, tm * ROW_TILES, LANES), F32), pltpu.SemaphoreType.DMA(())],
        compiler_params=_cp(("arbitrary",)),
        name="moe_combine",
    )(dest_flat, rgate, h, pe, yb, gp, pw, pg)


def _moe_and_ple(h, xn, ridx, rgate, pe, W):
    n, d = h.shape
    m = n * TOP_K
    blk = max(16, min(512, (m // N_EXPERTS) // 16 * 16))
    n_blk = -(-m // blk) + N_EXPERTS
    rpos, counts = _route(ridx)
    counts = counts[0].astype(I32)
    padded = (counts + blk - 1) // blk * blk
    pad_end = jnp.cumsum(padded)
    pad_start = pad_end - padded
    dest = pad_start[ridx[:, :TOP_K]] + rpos[:, :TOP_K]
    dest_flat = dest.reshape(m).astype(I32)
    blk_row0 = jnp.arange(n_blk, dtype=I32) * blk
    blk_e = jnp.minimum(jnp.sum((pad_end[None, :] <= blk_row0[:, None]).astype(I32), axis=1), N_EXPERTS - 1)
    nused = (pad_end[-1] // blk).astype(I32).reshape(1)
    blk_e = jnp.where(jnp.arange(n_blk) < nused[0], blk_e, blk_e[jnp.maximum(nused[0] - 1, 0)])
    tail_rows = jnp.where(counts % blk != 0, pad_end - blk, -1).astype(I32)
    xn_tiles = xn.reshape(n * ROW_TILES, LANES)
    xs = _dispatch(xn_tiles, dest_flat, tail_rows, nused, n_blk, blk)
    yb = _experts(xs, blk_e, nused, W['moe_w_gu'], W['moe_b_gu'], W['moe_w_down'], W['moe_b_down'], blk)
    return _combine(h, pe, yb, dest_flat, rgate, W)


def _layer_common(h, z, o_a, o_b, o_c, pe, W):
    h_mid, xn, ridx, rgate = _merge(h, o_a, o_b, o_c, z, W)
    return _moe_and_ple(h_mid, xn, ridx, rgate, pe, W)


def _layer_prompt(h, pe, wp, W, bsz, t_len, lam_init):
    n = bsz * t_len
    z = _in_proj(h, W['norm_mix_g'], wp)
    L = min(64, t_len)
    shift0 = jnp.zeros((bsz, C_RWKV), F32)
    wkv0 = jnp.zeros((bsz, H_A, D_HEAD, D_HEAD), F32)
    o_a, wkv_new, shift_new = _rwkv(z, shift0, wkv0, W, bsz, t_len, L)
    qn, kn = _qknorm(z, W['diff_q_norm'], W['diff_k_norm'])
    o_b = _diff_attn(qn, kn, z, W['diff_lambda'], W['diff_subln_g'], bsz, t_len, lam_init)
    o_c = _sb_attn(z, bsz, t_len)
    h = _layer_common(h, z, o_a, o_b, o_c, pe, W)
    rows = (kn.reshape(bsz, t_len, H_B, 2 * D_HEAD),
            z[:, VB0:VB0 + C_B].reshape(bsz, t_len, H_B, 2 * D_HEAD),
            z[:, KC0:KC0 + C_C].reshape(bsz, t_len, H_C, D_HEAD),
            z[:, VC0:VC0 + C_C].reshape(bsz, t_len, H_C, D_HEAD),
            wkv_new, shift_new)
    return h, rows


DEC_CHUNK = 16


def _layer_sample(h, pe, wp, W, caches, page_table, shift0, wkv0, layer, lam_init):
    bsz = h.shape[0]
    z = _in_proj(h, W['norm_mix_g'], wp)
    za = jnp.pad(z[:, None, :ZA_W], ((0, 0), (0, DEC_CHUNK - 1), (0, 0))).reshape(bsz * DEC_CHUNK, ZA_W)
    o_a, wkv_new, shift_new = _rwkv(za, shift0, wkv0, W, bsz, 1, DEC_CHUNK)
    o_a = o_a.reshape(bsz, DEC_CHUNK, C_A)[:, 0]
    qn, kn = _qknorm(z, W['diff_q_norm'], W['diff_k_norm'])
    vb = z[:, VB0:VB0 + C_B]
    o_b = _diff_dec(qn, kn, vb, caches[0], caches[1], page_table, layer, W['diff_lambda'], W['diff_subln_g'], lam_init)
    o_c = _sb_dec(z[:, QC0:QC0 + C_C], caches[2], caches[3], page_table, layer)
    h = _layer_common(h, z, o_a, o_b, o_c, pe, W)
    rows = (kn.reshape(bsz, 1, H_B, 2 * D_HEAD), vb.reshape(bsz, 1, H_B, 2 * D_HEAD),
            z[:, KC0:KC0 + C_C].reshape(bsz, 1, H_C, D_HEAD), z[:, VC0:VC0 + C_C].reshape(bsz, 1, H_C, D_HEAD),
            wkv_new, shift_new)
    return h, rows


def kernel(x_prompt, x_sample, cache_diff_k, cache_diff_v, cache_sb_k, cache_sb_v, state_wkv, state_shift, page_table, p_prompt, p_sample, norm_mix_g, w_in, rwkv_mu, rwkv_w0, rwkv_w2, rwkv_a0, rwkv_a2, rwkv_g2, rwkv_k_k, rwkv_k_a, rwkv_r_k, rwkv_lnx_g, rwkv_lnx_b, diff_q_norm, diff_k_norm, diff_lambda, diff_subln_g, w_branch_a, w_branch_b, w_branch_c, w_out, norm_ffn_g, router_w, router_b, moe_w_gu, moe_b_gu, moe_w_down, moe_b_down, norm_ple_g, ple_w, ple_gate_w):
    bsz_p, seq_p, d = x_prompt.shape
    bsz_s, seq_s, _ = x_sample.shape
    assert seq_s == 1 and d == ROW_TILES * LANES
    depth = w_in.shape[0]
    params = dict(norm_mix_g=norm_mix_g, rwkv_mu=rwkv_mu, rwkv_w0=rwkv_w0, rwkv_w2=rwkv_w2, rwkv_a0=rwkv_a0,
                  rwkv_a2=rwkv_a2, rwkv_g2=rwkv_g2, rwkv_k_k=rwkv_k_k, rwkv_k_a=rwkv_k_a, rwkv_r_k=rwkv_r_k,
                  rwkv_lnx_g=rwkv_lnx_g, rwkv_lnx_b=rwkv_lnx_b, diff_q_norm=diff_q_norm, diff_k_norm=diff_k_norm,
                  diff_lambda=diff_lambda, diff_subln_g=diff_subln_g, w_branch_a=w_branch_a,
                  w_branch_b=w_branch_b, w_branch_c=w_branch_c, w_out=w_out, norm_ffn_g=norm_ffn_g,
                  router_w=router_w, router_b=router_b, moe_w_gu=moe_w_gu, moe_b_gu=moe_b_gu,
                  moe_w_down=moe_w_down, moe_b_down=moe_b_down, norm_ple_g=norm_ple_g, ple_w=ple_w,
                  ple_gate_w=ple_gate_w)
    caches = (cache_diff_k, cache_diff_v, cache_sb_k, cache_sb_v)
    h_p = x_prompt.reshape(bsz_p * seq_p, d)
    h_s = x_sample.reshape(bsz_s, d)
    rows_p, rows_s = [], []
    for i in range(depth):
        W = {name: val[i] for name, val in params.items()}
        wp = _pack_w_in(w_in[i])
        lam_init = 0.8 - 0.6 * math.exp(-0.3 * i)
        h_p, r_p = _layer_prompt(h_p, p_prompt[i].reshape(bsz_p * seq_p, -1), wp, W, bsz_p, seq_p, lam_init)
        h_s, r_s = _layer_sample(h_s, p_sample[i].reshape(bsz_s, -1), wp, W, caches, page_table,
                                 state_shift[:, i], state_wkv[:, i], i, lam_init)
        rows_p.append(r_p)
        rows_s.append(r_s)
    st = lambda rows, j: jnp.stack([r[j] for r in rows], axis=1)
    return (h_p.reshape(bsz_p, seq_p, d), h_s.reshape(bsz_s, seq_s, d),
            st(rows_p, 0), st(rows_p, 1), st(rows_p, 2), st(rows_p, 3), st(rows_p, 4), st(rows_p, 5),
            st(rows_s, 0), st(rows_s, 1), st(rows_s, 2), st(rows_s, 3), st(rows_s, 4), st(rows_s, 5))
```

```python
import functools
import math

import numpy as np
import jax
import jax.numpy as jnp
from jax import lax
from jax.experimental import pallas as pl
from jax.experimental.pallas import tpu as pltpu

F32 = jnp.float32
BF16 = jnp.bfloat16
I32 = jnp.int32

D_HEAD = 64
H_A, H_B, H_C = 8, 4, 6
C_A, C_B, C_C = H_A * D_HEAD, H_B * 2 * D_HEAD, H_C * D_HEAD
D_DECAY_LORA, D_AAA_LORA, D_GATE_LORA = 64, 64, 128
C_RWKV = 3 * C_A + D_DECAY_LORA + D_AAA_LORA + D_GATE_LORA
N_EXPERTS, TOP_K = 32, 4
SWIGLU_LIMIT, SWIGLU_ALPHA = 7.0, 1.702
PAGE_SIZE = 128
RMS_EPS = 1e-6
GN_EPS = 64e-5

LANES = 128
SUBLANES = 8
ROW_TILES = 8

ZA0, ZA_W = 0, 2048
QB0, KB0, VB0 = 2048, 2560, 3072
QC0, KC0, VC0 = 3584, 4096, 4608
ZG0 = 5120
ZW = 8192

SB_DEAD = -110.0
NEG_BIG = -1e30

NN = (((1,), (0,)), ((), ()))
NT = (((1,), (1,)), ((), ()))
TN = (((0,), (0,)), ((), ()))

VMEM_LIMIT = 56 * 1024 * 1024


def _cp(sem, vmem=None):
    return pltpu.CompilerParams(dimension_semantics=sem, vmem_limit_bytes=vmem)


def _dot(a, b, dims=NN):
    return lax.dot_general(a, b, dims, preferred_element_type=F32)


def _split(x):
    hi = x.astype(BF16)
    lo = (x - hi.astype(F32)).astype(BF16)
    return hi, lo


def _mm(a, b, dims=NN, passes=1):
    if passes == 1:
        return _dot(a.astype(BF16), b.astype(BF16), dims)
    ah, al = _split(a)
    bh, bl = _split(b)
    return _dot(ah, bh, dims) + (_dot(ah, bl, dims) + _dot(al, bh, dims))


def _mm_exact_rhs(a, b_exact, passes=2):
    hi = a.astype(BF16)
    out = _dot(hi, b_exact)
    rem = a - hi.astype(F32)
    for _ in range(passes - 1):
        part = rem.astype(BF16)
        out = out + _dot(part, b_exact)
        rem = rem - part.astype(F32)
    return out


def _mm_exact_lhs(a_exact, b, passes=2):
    hi = b.astype(BF16)
    out = _dot(a_exact, hi)
    rem = b - hi.astype(F32)
    for _ in range(passes - 1):
        part = rem.astype(BF16)
        out = out + _dot(a_exact, part)
        rem = rem - part.astype(F32)
    return out


def _sigmoid(x):
    return 1.0 / (1.0 + jnp.exp(-x))


def _log_sigmoid_pair(z):
    lg = jnp.log(1.0 + jnp.exp(-jnp.abs(z)))
    return jnp.minimum(z, 0.0) - lg, -jnp.maximum(z, 0.0) - lg


def _block_indicator(n, blk):
    idx = np.arange(n) // blk
    return jnp.asarray((idx[:, None] == idx[None, :]).astype(np.float32), dtype=BF16)


IN_PROJ_TN = 512
IN_PROJ_SUB = IN_PROJ_TN // LANES


def _in_proj_kernel(src_ref, x_ref, g_ref, *rest):
    w_refs, (o_ref, xn_ref) = rest[:IN_PROJ_SUB], rest[IN_PROJ_SUB:]

    @pl.when(pl.program_id(1) == 0)
    def _():
        x = x_ref[...]
        ms = jnp.mean(x * x, axis=-1, keepdims=True)
        xn_ref[...] = (x * lax.rsqrt(ms + RMS_EPS) * g_ref[...]).astype(BF16)

    w = jnp.concatenate([w_ref[...].astype(BF16) for w_ref in w_refs], axis=1)
    o_ref[...] = _dot(xn_ref[...], w)


def _in_proj_columns():
    src = []

    def slot(first, count, width):
        blocks = list(range(first, first + count))
        src.extend(blocks + [blocks[-1]] * (width - count))

    c_diff = 3 * C_B
    slot(0, C_RWKV // LANES, ZA_W // LANES)
    slot(C_RWKV // LANES, c_diff // LANES, c_diff // LANES)
    for part in range(3):
        slot((C_RWKV + c_diff + part * C_C) // LANES, C_C // LANES, 512 // LANES)
    gate0 = (C_RWKV + c_diff + 3 * C_C) // LANES
    slot(gate0, (ZW - ZG0) // LANES, (ZW - ZG0) // LANES)
    assert len(src) == ZW // LANES
    return jnp.asarray(src, I32)


def _in_proj(x, g, w_in):
    n, d = x.shape
    tm = min(1024, n)
    w_all, layer = w_in
    w_spec = lambda k: pl.BlockSpec((None, d, LANES), lambda i, j, src: (layer, 0, src[j * IN_PROJ_SUB + k]))
    grid_spec = pltpu.PrefetchScalarGridSpec(
        num_scalar_prefetch=1,
        grid=(n // tm, ZW // IN_PROJ_TN),
        in_specs=[pl.BlockSpec((tm, d), lambda i, j, src: (i, 0)),
                  pl.BlockSpec((1, d), lambda i, j, src: (0, 0))] + [w_spec(k) for k in range(IN_PROJ_SUB)],
        out_specs=pl.BlockSpec((tm, IN_PROJ_TN), lambda i, j, src: (i, j)),
        scratch_shapes=[pltpu.VMEM((tm, d), BF16)])
    return pl.pallas_call(
        _in_proj_kernel, grid_spec=grid_spec,
        out_shape=jax.ShapeDtypeStruct((n, ZW), F32),
        compiler_params=_cp(("parallel", "arbitrary")),
        name="in_proj",
    )(_in_proj_columns(), x, g.reshape(1, d), *([w_all] * IN_PROJ_SUB))


RWKV_GROUP = 4
RWKV_PASSES = dict(tinv=1, pu=1, out=1, state=3)
RWKV_SEQS_PER_STEP = 2


def _bd(y, mask):
    yb = y.astype(BF16)
    return jnp.concatenate([yb] * RWKV_GROUP, axis=0) * mask


def _mm_bd(x, y, mask, passes, dims=NN):
    if passes == 1:
        return _dot(x.astype(BF16), _bd(y, mask), dims)
    xh, xl = _split(x)
    yh, yl = _split(y)
    bh, bl = _bd(yh, mask), _bd(yl, mask)
    return _dot(xh, bh, dims) + (_dot(xh, bl, dims) + _dot(xl, bh, dims))


def _tri_inv_cat(a, n, eye, same_blk, mask, passes):
    bs = min(16, n)
    if n > bs:
        ad = jnp.where(same_blk, a, 0.0)
        ao = a - ad
    else:
        ad, ao = a, None
    td = eye + ad
    pw = ad
    p = 2
    while p < bs:
        pw = _mm_bd(pw, pw, mask, passes)
        td = td + _mm_bd(td, pw, mask, passes)
        p *= 2
    if ao is None:
        return td
    nmat = _mm_bd(td, ao, mask, passes)
    res = eye + nmat
    npw = nmat
    p = 2
    while p < n // bs:
        npw = _mm_bd(npw, npw, mask, passes)
        res = res + _mm_bd(res, npw, mask, passes)
        p *= 2
    return _mm_bd(res, td, mask, passes)


def _rwkv_seq(bi, c, z_ref, mu_ref, vec_ref, w2_ref, a2_ref, g2_ref, bd_ref, mch_ref, mll_ref,
              o_ref, carry_ref, state_ref, *, L, nc, t_real):
    z = z_ref[bi]
    row1 = lax.broadcasted_iota(I32, (L, 1), 0)
    zp = jnp.where(row1 == 0, carry_ref[bi], pltpu.roll(z, 1, axis=0))
    carry_ref[bi] = z[L - 1:L, :]
    zs = z + (zp - z) * mu_ref[...]

    r = zs[:, 0:C_A]
    k = zs[:, C_A:2 * C_A]
    v = zs[:, 2 * C_A:3 * C_A]
    o1 = 3 * C_A
    w_lo = zs[:, o1:o1 + D_DECAY_LORA]
    a_lo = zs[:, o1 + D_DECAY_LORA:o1 + D_DECAY_LORA + D_AAA_LORA]
    g_lo = zs[:, o1 + D_DECAY_LORA + D_AAA_LORA:C_RWKV]

    w0, a0 = vec_ref[0:1, :], vec_ref[1:2, :]
    k_k, k_a, r_k = vec_ref[2:3, :], vec_ref[3:4, :], vec_ref[4:5, :]
    lnx_g, lnx_b = vec_ref[5:6, :], vec_ref[6:7, :]
    bd = bd_ref[...]

    xw = w0 + _mm(jnp.tanh(w_lo), w2_ref[...])
    w_log = -(jnp.maximum(-xw, 0.0) + jnp.log(1.0 + jnp.exp(-jnp.abs(xw)))) - 0.5
    logw = -jnp.exp(w_log)
    a = _sigmoid(a0 + _mm(a_lo, a2_ref[...]))
    g = _mm(_sigmoid(g_lo), g2_ref[...])
    kk = k * k_k
    ss = _mm_exact_rhs(kk * kk, bd, 2)
    kk = kk / jnp.maximum(jnp.sqrt(ss), 1e-12)
    k2 = k * (1.0 + (a - 1.0) * k_a)
    if nc * L != t_real:
        valid = (c * L + row1) < t_real
        logw = jnp.where(valid, logw, 0.0)
        kk = jnp.where(valid, kk, 0.0)
        k2 = jnp.where(valid, k2, 0.0)
        v = jnp.where(valid, v, 0.0)

    row = lax.broadcasted_iota(I32, (L, L), 0)
    col = lax.broadcasted_iota(I32, (L, L), 1)
    cs = _mm_exact_lhs(jnp.where(col <= row, 1.0, 0.0).astype(BF16), logw, 3)
    cs_l = cs[L - 1:L, :]
    e_in = jnp.exp(cs)
    e_ex = jnp.exp(cs - logw)
    e_neg = jnp.exp(-cs)
    e_end = jnp.exp(cs_l - cs)
    w_l = e_in[L - 1:L, :]
    kka = kk * a
    at = -kk * e_ex
    rt = r * e_in
    bt = kka * e_neg
    kt = k2 * e_neg
    bh = kka * e_end
    kh = k2 * e_end

    rk_sum = _mm_exact_rhs(r * k2 * r_k, bd, 2)
    bonus = rk_sum * v

    gw = RWKV_GROUP * D_HEAD
    gl = RWKV_GROUP * L
    t_i = lax.broadcasted_iota(I32, (L, gl), 0)
    j_i = lax.broadcasted_iota(I32, (L, gl), 1) % L
    tril_incl = j_i <= t_i
    tril_strict = j_i < t_i
    eye_cat = jnp.where(j_i == t_i, 1.0, 0.0).astype(F32)
    same_blk = (t_i // 16) == (j_i // 16)
    mch = mch_ref[...]
    mll = mll_ref[...]
    bdf = bd[:gw, :gw].astype(F32)
    diag = lax.broadcasted_iota(I32, (gw, gw), 0) == lax.broadcasted_iota(I32, (gw, gw), 1)
    pp = RWKV_PASSES

    ys = []
    for gi in range(H_A // RWKV_GROUP):
        sl = slice(gi * gw, (gi + 1) * gw)
        at_g, rt_g, bt_g, kt_g, bh_g, kh_g, v_g = at[:, sl], rt[:, sl], bt[:, sl], kt[:, sl], bh[:, sl], kh[:, sl], v[:, sl]
        ar = jnp.concatenate([at_g, rt_g], axis=0).astype(BF16)
        g_b = _dot(ar, _bd(bt_g, mch), NT)
        g_k = _dot(ar, _bd(kt_g, mch), NT)
        a_ab = jnp.where(tril_strict, g_b[:L], 0.0)
        a_rb = jnp.where(tril_incl, g_b[L:], 0.0)
        a_ak = jnp.where(tril_strict, g_k[:L], 0.0)
        a_rk = jnp.where(tril_incl, g_k[L:], 0.0)
        tinv = _tri_inv_cat(a_ab, L, eye_cat, same_blk, mll, pp['tinv'])
        v_bd = _bd(v_g, mch)
        av = _dot(a_ak.astype(BF16), v_bd)
        p_m = _mm_bd(tinv, at_g, mch, pp['pu'])
        u0 = _mm_bd(tinv, av, mch, pp['pu'])
        y0 = _dot(a_rk.astype(BF16), v_bd) + _mm_bd(a_rb, u0, mch, 1)
        q_m = rt_g + _mm_bd(a_rb, p_m, mch, 1)
        s_bd = state_ref[bi, gi]
        ys.append(y0 + _mm(q_m, s_bd, NN, pp['out']))
        m_t = _mm(bh_g, p_m, TN, pp['state']) * bdf + jnp.where(diag, w_l[:, sl], 0.0)
        c_t = _mm(jnp.concatenate([bh_g, kh_g], axis=0), jnp.concatenate([u0, v_g], axis=0), TN, pp['state']) * bdf
        state_ref[bi, gi] = _mm(m_t, s_bd, NN, pp['state']) + c_t
    y = jnp.concatenate(ys, axis=1)

    mean = _mm_exact_rhs(y, bd, 2) * (1.0 / D_HEAD)
    dlt = y - mean
    var = _mm_exact_rhs(dlt * dlt, bd, 2) * (1.0 / D_HEAD)
    yn = dlt * lax.rsqrt(var + GN_EPS)
    o_ref[bi] = (yn * lnx_g + lnx_b + bonus) * g


def _rwkv_kernel(z_ref, shift0_ref, sbd0_ref, mu_ref, vec_ref, w2_ref, a2_ref, g2_ref, bd_ref, mch_ref, mll_ref,
                 o_ref, sbd_out_ref, shift_out_ref, carry_ref, state_ref, *, L, nc, t_real, bb):
    c = pl.program_id(1)

    @pl.when(c == 0)
    def _():
        carry_ref[...] = shift0_ref[...]
        state_ref[...] = sbd0_ref[...]

    for bi in range(bb):
        _rwkv_seq(bi, c, z_ref, mu_ref, vec_ref, w2_ref, a2_ref, g2_ref, bd_ref, mch_ref, mll_ref,
                  o_ref, carry_ref, state_ref, L=L, nc=nc, t_real=t_real)

    @pl.when(c == nc - 1)
    def _():
        sbd_out_ref[...] = state_ref[...]
        rl = t_real - 1 - (nc - 1) * L
        shift_out_ref[...] = z_ref[:, rl:rl + 1, :]


def _rwkv(z_rows, shift0, wkv0, W, bsz, t_real, L):
    nc = -(-t_real // L)
    pad = lambda a: jnp.pad(a, ((0, 0), (0, ZA_W - a.shape[-1])))
    mu = pad(W['rwkv_mu'].reshape(1, C_RWKV))
    vec = jnp.concatenate([W['rwkv_w0'][None], W['rwkv_a0'][None], W['rwkv_k_k'][None], W['rwkv_k_a'][None],
                           W['rwkv_r_k'].reshape(1, C_A), W['rwkv_lnx_g'][None], W['rwkv_lnx_b'][None],
                           jnp.zeros((1, C_A), F32)], axis=0)
    shift0p = pad(shift0).reshape(bsz, 1, ZA_W)
    hg, ng = RWKV_GROUP, H_A // RWKV_GROUP
    gw, gl = hg * D_HEAD, hg * L
    eye_h = jnp.eye(hg, dtype=F32)
    st = jnp.swapaxes(wkv0, -1, -2).reshape(bsz, ng, hg, D_HEAD, D_HEAD)
    sbd0 = (st[:, :, :, :, None, :] * eye_h[None, None, :, None, :, None]).reshape(bsz, ng, gw, gw)
    rblk = np.arange(gl) // L
    mch = jnp.asarray((rblk[:, None] == (np.arange(gw) // D_HEAD)[None, :]).astype(np.float32), dtype=BF16)
    mll = jnp.asarray((rblk[:, None] == rblk[None, :]).astype(np.float32), dtype=BF16)
    bb = RWKV_SEQS_PER_STEP if bsz % RWKV_SEQS_PER_STEP == 0 else 1
    kern = functools.partial(_rwkv_kernel, L=L, nc=nc, t_real=t_real, bb=bb)
    full = lambda shape: pl.BlockSpec(shape, lambda b, c: (0,) * len(shape))
    o_a, sbd_new, shift_new = pl.pallas_call(
        kern,
        grid=(bsz // bb, nc),
        in_specs=[pl.BlockSpec((bb, L, ZA_W), lambda b, c: (b, c, 0)),
                  pl.BlockSpec((bb, 1, ZA_W), lambda b, c: (b, 0, 0)),
                  pl.BlockSpec((bb, ng, gw, gw), lambda b, c: (b, 0, 0, 0)),
                  full((1, ZA_W)), full((8, C_A)), full((D_DECAY_LORA, C_A)), full((D_AAA_LORA, C_A)),
                  full((D_GATE_LORA, C_A)), full((C_A, C_A)), full((gl, gw)), full((gl, gl))],
        out_specs=[pl.BlockSpec((bb, L, C_A), lambda b, c: (b, c, 0)),
                   pl.BlockSpec((bb, ng, gw, gw), lambda b, c: (b, 0, 0, 0)),
                   pl.BlockSpec((bb, 1, ZA_W), lambda b, c: (b, 0, 0))],
        out_shape=[jax.ShapeDtypeStruct((bsz, nc * L, C_A), F32),
                   jax.ShapeDtypeStruct((bsz, ng, gw, gw), F32),
                   jax.ShapeDtypeStruct((bsz, 1, ZA_W), F32)],
        scratch_shapes=[pltpu.VMEM((bb, 1, ZA_W), F32), pltpu.VMEM((bb, ng, gw, gw), F32)],
        compiler_params=_cp(("parallel", "arbitrary"), VMEM_LIMIT),
        name="rwkv",
    )(z_rows.reshape(bsz, nc * L, z_rows.shape[-1]), shift0p, sbd0, mu, vec, W['rwkv_w2'].astype(BF16),
      W['rwkv_a2'].astype(BF16), W['rwkv_g2'].astype(BF16), _block_indicator(C_A, D_HEAD), mch, mll)
    wkv_new = jnp.einsum('bghchv->bghvc', sbd_new.reshape(bsz, ng, hg, D_HEAD, hg, D_HEAD))
    wkv_new = wkv_new.reshape(bsz, H_A, D_HEAD, D_HEAD)
    return o_a.reshape(bsz * nc * L, C_A), wkv_new, shift_new[:, 0, :C_RWKV]


def _qknorm_kernel(q_ref, k_ref, gq_ref, gk_ref, bd_ref, qn_ref, kn_ref):
    bd = bd_ref[...]
    q = q_ref[...]
    k = k_ref[...]
    msq = _mm_exact_rhs(q * q, bd, 2) * (1.0 / D_HEAD)
    msk = _mm_exact_rhs(k * k, bd, 2) * (1.0 / D_HEAD)
    qn_ref[...] = q * lax.rsqrt(msq + RMS_EPS) * gq_ref[...] * (D_HEAD ** -0.5)
    kn_ref[...] = k * lax.rsqrt(msk + RMS_EPS) * gk_ref[...]


def _qknorm(z, gq, gk):
    n = z.shape[0]
    tm = min(512, n)
    tile = lambda g: jnp.tile(g.reshape(1, D_HEAD), (1, C_B // D_HEAD))
    return pl.pallas_call(
        _qknorm_kernel,
        grid=(n // tm,),
        in_specs=[pl.BlockSpec((tm, C_B), lambda i: (i, QB0 // C_B)),
                  pl.BlockSpec((tm, C_B), lambda i: (i, KB0 // C_B)),
                  pl.BlockSpec((1, C_B), lambda i: (0, 0)),
                  pl.BlockSpec((1, C_B), lambda i: (0, 0)),
                  pl.BlockSpec((C_B, C_B), lambda i: (0, 0))],
        out_specs=[pl.BlockSpec((tm, C_B), lambda i: (i, 0)), pl.BlockSpec((tm, C_B), lambda i: (i, 0))],
        out_shape=[jax.ShapeDtypeStruct((n, C_B), F32), jax.ShapeDtypeStruct((n, C_B), F32)],
        compiler_params=_cp(("parallel",)),
        name="qknorm",
    )(z, z, tile(gq), tile(gk), _block_indicator(C_B, D_HEAD))


def _lam_value(lam_ref, lam_init):
    lv = lam_ref[...]
    l1 = jnp.exp(jnp.sum(lv[0:1, :] * lv[1:2, :], axis=-1, keepdims=True))
    l2 = jnp.exp(jnp.sum(lv[2:3, :] * lv[3:4, :], axis=-1, keepdims=True))
    return l1 - l2 + lam_init


def _diff_attn_kernel(qi_tab, kj_tab, q_ref, k_ref, v_ref, lam_ref, g_ref, o_ref,
                      qs_ref, m_ref, l_ref, acc_ref, *, tq, tk, rg, lam_init):
    p = pl.program_id(2)
    qi = qi_tab[p]
    kj = kj_tab[p]

    @pl.when(kj == 0)
    def _():
        q = q_ref[...]
        lane = lax.broadcasted_iota(I32, q.shape, 1)
        qs_ref[0:tq, :] = jnp.where(lane < D_HEAD, q, 0.0).astype(BF16)
        qs_ref[tq:2 * tq, :] = jnp.where(lane >= D_HEAD, q, 0.0).astype(BF16)
        m_ref[...] = jnp.full(m_ref.shape, NEG_BIG, F32)
        l_ref[...] = jnp.zeros(l_ref.shape, F32)
        acc_ref[...] = jnp.zeros(acc_ref.shape, F32)

    def body(masked):
        k = k_ref[...].astype(BF16)
        v = v_ref[...].astype(BF16)
        for g in range(2 * tq // rg):
            cs = pl.ds(g * rg, rg)
            s = _dot(k, qs_ref[cs, :], NT)
            if masked:
                kpos = kj * tk + lax.broadcasted_iota(I32, (tk, rg), 0)
                qpos = qi * tq + (g * rg) % tq + lax.broadcasted_iota(I32, (tk, rg), 1)
                s = jnp.where(kpos <= qpos, s, NEG_BIG)
            m_prev = m_ref[:, cs]
            m_new = jnp.maximum(m_prev, jnp.max(s, axis=0, keepdims=True))
            alpha = jnp.exp(m_prev - m_new)
            pr = jnp.exp(s - m_new)
            l_ref[:, cs] = alpha * l_ref[:, cs] + jnp.sum(pr, axis=0, keepdims=True)
            acc_ref[:, cs] = alpha * acc_ref[:, cs] + _dot(v, pr.astype(BF16), TN)
            m_ref[:, cs] = m_new

    crosses = (kj + 1) * tk - 1 > qi * tq
    pl.when(crosses)(lambda: body(True))
    pl.when(jnp.logical_not(crosses))(lambda: body(False))

    @pl.when(kj == ((qi + 1) * tq - 1) // tk)
    def _():
        lam = _lam_value(lam_ref, lam_init)
        acc = acc_ref[...]
        inv_l = 1.0 / l_ref[...]
        o = acc[:, :tq] * inv_l[:, :tq] - lam * (acc[:, tq:] * inv_l[:, tq:])
        ms = jnp.mean(o * o, axis=0, keepdims=True)
        o = o * lax.rsqrt(ms + RMS_EPS) * g_ref[...] * (1.0 - lam_init)
        o_ref[...] = o.T


DIFF_ROW_GROUP = 128


def _causal_pairs(nq, tq, tk, descending):
    qi_l, kj_l = [], []
    for qi in range(nq):
        last = ((qi + 1) * tq - 1) // tk
        ks = range(last, -1, -1) if descending else range(last + 1)
        for kj in ks:
            qi_l.append(qi)
            kj_l.append(kj)
    return jnp.asarray(qi_l, I32), jnp.asarray(kj_l, I32)


def _diff_attn(qn, kn, z, lam_w, subln_g, bsz, t_len, lam_init):
    tq = tk = min(512, t_len)
    nq, nk = t_len // tq, t_len // tk
    qi_tab, kj_tab = _causal_pairs(nq, tq, tk, False)
    vcol = VB0 // LANES
    kern = functools.partial(_diff_attn_kernel, tq=tq, tk=tk, rg=min(DIFF_ROW_GROUP, tq), lam_init=lam_init)
    grid_spec = pltpu.PrefetchScalarGridSpec(
        num_scalar_prefetch=2,
        grid=(bsz, H_B, int(qi_tab.shape[0])),
        in_specs=[pl.BlockSpec((tq, LANES), lambda b, h, p, qt, kt: (b * nq + qt[p], h)),
                  pl.BlockSpec((tk, LANES), lambda b, h, p, qt, kt: (b * nk + kt[p], h)),
                  pl.BlockSpec((tk, LANES), lambda b, h, p, qt, kt: (b * nk + kt[p], vcol + h)),
                  pl.BlockSpec((4, D_HEAD), lambda b, h, p, qt, kt: (0, 0)),
                  pl.BlockSpec((LANES, 1), lambda b, h, p, qt, kt: (0, 0))],
        out_specs=pl.BlockSpec((tq, LANES), lambda b, h, p, qt, kt: (b * nq + qt[p], h)),
        scratch_shapes=[pltpu.VMEM((2 * tq, LANES), BF16), pltpu.VMEM((1, 2 * tq), F32),
                        pltpu.VMEM((1, 2 * tq), F32), pltpu.VMEM((LANES, 2 * tq), F32)])
    return pl.pallas_call(
        kern, grid_spec=grid_spec,
        out_shape=jax.ShapeDtypeStruct((bsz * t_len, C_B), F32),
        compiler_params=_cp(("parallel", "parallel", "arbitrary")),
        name="diff_attn",
    )(qi_tab, kj_tab, qn, kn, z, lam_w, subln_g.reshape(2 * D_HEAD, 1))


def _sb_attn_kernel(qi_tab, kj_tab, q_ref, k_ref, v_ref, o_ref, qs_ref, c_ref, acc_ref, done_ref,
                    *, tq, tkb, sub):
    p = pl.program_id(2)
    qi = qi_tab[p]
    kj = kj_tab[p]
    first_kj = ((qi + 1) * tq - 1) // tkb

    @pl.when(kj == first_kj)
    def _():
        q = q_ref[...] * (D_HEAD ** -0.5)
        lane = lax.broadcasted_iota(I32, q.shape, 1)
        qs_ref[0:tq, :] = jnp.where(lane < D_HEAD, q, 0.0).astype(BF16)
        qs_ref[tq:2 * tq, :] = jnp.where(lane >= D_HEAD, q, 0.0).astype(BF16)
        c_ref[...] = jnp.zeros(c_ref.shape, F32)
        acc_ref[...] = jnp.zeros(acc_ref.shape, F32)
        done_ref[0] = 0

    rr = lax.broadcasted_iota(I32, (sub, sub), 0)
    cc = lax.broadcasted_iota(I32, (sub, sub), 1)
    upper = jnp.where(rr > cc, 1.0, 0.0).astype(BF16)

    def body(masked):
        qs = qs_ref[...]
        for sb in reversed(range(tkb // sub)):
            ksl = slice(sb * sub, (sb + 1) * sub)
            z = _dot(qs, k_ref[ksl, :].astype(BF16), NT)
            logb, l1 = _log_sigmoid_pair(z)
            if masked:
                qpos = qi * tq + lax.broadcasted_iota(I32, (tq, sub), 0)
                kpos = kj * tkb + sb * sub + lax.broadcasted_iota(I32, (tq, sub), 1)
                ok = kpos < qpos
                ok = jnp.concatenate([ok, ok], axis=0)
                l1 = jnp.where(ok, l1, 0.0)
            suf = _mm_exact_rhs(l1, upper, 2)
            c_prev = c_ref[...]
            att = jnp.exp(logb + suf + c_prev)
            if masked:
                att = jnp.where(ok, att, 0.0)
            acc_ref[...] += _dot(att.astype(BF16), v_ref[ksl, :].astype(BF16))
            c_ref[...] = c_prev + jnp.sum(l1, axis=-1, keepdims=True)
        done_ref[0] = (jnp.max(c_ref[...]) < SB_DEAD).astype(I32)

    live = done_ref[0] == 0
    crosses = (kj + 1) * tkb > qi * tq
    pl.when(jnp.logical_and(live, crosses))(lambda: body(True))
    pl.when(jnp.logical_and(live, jnp.logical_not(crosses)))(lambda: body(False))

    @pl.when(kj == 0)
    def _():
        acc = acc_ref[...]
        lane = lax.broadcasted_iota(I32, (tq, LANES), 1)
        o_ref[...] = jnp.where(lane < D_HEAD, acc[:tq], acc[tq:])


def _sb_attn(z, bsz, t_len):
    tq = tkb = min(512, t_len)
    sub = min(128, tkb)
    nq, nk = t_len // tq, t_len // tkb
    qi_tab, kj_tab = _causal_pairs(nq, tq, tkb, True)
    qc, kc, vc = QC0 // LANES, KC0 // LANES, VC0 // LANES
    kern = functools.partial(_sb_attn_kernel, tq=tq, tkb=tkb, sub=sub)
    grid_spec = pltpu.PrefetchScalarGridSpec(
        num_scalar_prefetch=2,
        grid=(bsz, H_C // 2, int(qi_tab.shape[0])),
        in_specs=[pl.BlockSpec((tq, LANES), lambda b, h, p, qt, kt: (b * nq + qt[p], qc + h)),
                  pl.BlockSpec((tkb, LANES), lambda b, h, p, qt, kt: (b * nk + kt[p], kc + h)),
                  pl.BlockSpec((tkb, LANES), lambda b, h, p, qt, kt: (b * nk + kt[p], vc + h))],
        out_specs=pl.BlockSpec((tq, LANES), lambda b, h, p, qt, kt: (b * nq + qt[p], h)),
        scratch_shapes=[pltpu.VMEM((2 * tq, LANES), BF16), pltpu.VMEM((2 * tq, 1), F32),
                        pltpu.VMEM((2 * tq, LANES), F32), pltpu.SMEM((1,), I32)])
    return pl.pallas_call(
        kern, grid_spec=grid_spec,
        out_shape=jax.ShapeDtypeStruct((bsz * t_len, C_C), F32),
        compiler_params=_cp(("parallel", "parallel", "arbitrary")),
        name="sb_attn",
    )(qi_tab, kj_tab, z, z, z)


DEC_ROWS = 16


def _diff_dec_kernel(pt_ref, q_ref, ks_ref, vs_ref, lam_ref, g_ref, *rest, G, lam_init):
    k_refs, v_refs = rest[:G], rest[G:2 * G]
    o_ref, qm_ref, m_ref, l_ref, acc_ref = rest[2 * G:]
    j = pl.program_id(1)
    hd = 2 * D_HEAD
    rowi = lax.broadcasted_iota(I32, (DEC_ROWS, hd), 0)
    lane = lax.broadcasted_iota(I32, (DEC_ROWS, hd), 1)
    sel = (lane // D_HEAD) == rowi

    def head_q(h):
        return jnp.where(sel, q_ref[0][:, h * hd:(h + 1) * hd], 0.0)

    @pl.when(j == 0)
    def _():
        for h in range(H_B):
            qm_ref[h * DEC_ROWS:(h + 1) * DEC_ROWS, :] = head_q(h).astype(BF16)
        m_ref[...] = jnp.full(m_ref.shape, NEG_BIG, F32)
        l_ref[...] = jnp.zeros(l_ref.shape, F32)
        acc_ref[...] = jnp.zeros(acc_ref.shape, F32)

    qm = qm_ref[...]
    head_page = lambda ref, h: ref[pl.ds(h, PAGE_SIZE, stride=H_B), :].astype(BF16)
    s = jnp.concatenate(
        [jnp.concatenate([_dot(qm[h * DEC_ROWS:(h + 1) * DEC_ROWS], head_page(k_refs[gi], h), NT)
                          for gi in range(G)], axis=1) for h in range(H_B)], axis=0)
    m_prev = m_ref[...]
    m_new = jnp.maximum(m_prev, jnp.max(s, axis=-1, keepdims=True))
    alpha = jnp.exp(m_prev - m_new)
    pr = jnp.exp(s - m_new)
    l_ref[...] = alpha * l_ref[...] + jnp.sum(pr, axis=-1, keepdims=True)
    pr = pr.astype(BF16)
    pvs = []
    for h in range(H_B):
        pv = jnp.zeros((DEC_ROWS, hd), F32)
        for gi in range(G):
            pv = pv + _dot(pr[h * DEC_ROWS:(h + 1) * DEC_ROWS, gi * PAGE_SIZE:(gi + 1) * PAGE_SIZE],
                           head_page(v_refs[gi], h))
        pvs.append(pv)
    acc_ref[...] = alpha * acc_ref[...] + jnp.concatenate(pvs, axis=0)
    m_ref[...] = m_new

    @pl.when(j == pl.num_programs(1) - 1)
    def _():
        lam = _lam_value(lam_ref, lam_init)
        outs = []
        for h in range(H_B):
            rs = slice(h * DEC_ROWS, (h + 1) * DEC_ROWS)
            cs = slice(h * hd, (h + 1) * hd)
            s = jnp.sum(head_q(h) * ks_ref[0][:, cs], axis=-1, keepdims=True)
            m_prev = m_ref[rs, :]
            m_new = jnp.maximum(m_prev, s)
            alpha = jnp.exp(m_prev - m_new)
            pr = jnp.exp(s - m_new)
            l_fin = alpha * l_ref[rs, :] + pr
            att = (alpha * acc_ref[rs, :] + pr * vs_ref[0][:, cs]) / l_fin
            o = att[0:1, :] - lam * att[1:2, :]
            ms = jnp.mean(o * o, axis=-1, keepdims=True)
            outs.append(o * lax.rsqrt(ms + RMS_EPS))
        o_ref[0] = jnp.concatenate(outs, axis=-1) * g_ref[...] * (1.0 - lam_init)


def _pages_per_step(n_pages, g):
    while n_pages % g:
        g //= 2
    return g


def _diff_dec(qn, kn, vb, cache_k, cache_v, page_table, layer, lam_w, subln_g, lam_init):
    bsz, n_pages = page_table.shape
    G = _pages_per_step(n_pages, 16)
    nl = cache_k.shape[1]
    hd = 2 * D_HEAD
    ck = cache_k.reshape(cache_k.shape[0], nl, PAGE_SIZE * H_B, hd)
    cv = cache_v.reshape(cache_v.shape[0], nl, PAGE_SIZE * H_B, hd)
    row3 = lambda a: a.reshape(bsz, 1, C_B)
    vec_spec = pl.BlockSpec((1, 1, C_B), lambda b, j, pt: (b, 0, 0))
    page_spec = lambda gi: pl.BlockSpec((None, None, PAGE_SIZE * H_B, hd),
                                        lambda b, j, pt: (pt[b, j * G + gi], layer, 0, 0))
    kern = functools.partial(_diff_dec_kernel, G=G, lam_init=lam_init)
    grid_spec = pltpu.PrefetchScalarGridSpec(
        num_scalar_prefetch=1,
        grid=(bsz, n_pages // G),
        in_specs=[vec_spec, vec_spec, vec_spec,
                  pl.BlockSpec((4, D_HEAD), lambda b, j, pt: (0, 0)),
                  pl.BlockSpec((1, C_B), lambda b, j, pt: (0, 0))]
                 + [page_spec(gi) for gi in range(G)] + [page_spec(gi) for gi in range(G)],
        out_specs=pl.BlockSpec((1, 1, C_B), lambda b, j, pt: (b, 0, 0)),
        scratch_shapes=[pltpu.VMEM((H_B * DEC_ROWS, hd), BF16), pltpu.VMEM((H_B * DEC_ROWS, 1), F32),
                        pltpu.VMEM((H_B * DEC_ROWS, 1), F32), pltpu.VMEM((H_B * DEC_ROWS, hd), F32)])
    out = pl.pallas_call(
        kern, grid_spec=grid_spec,
        out_shape=jax.ShapeDtypeStruct((bsz, 1, C_B), F32),
        compiler_params=_cp(("parallel", "arbitrary")),
        name="diff_dec",
    )(page_table, row3(qn), row3(kn), row3(vb), lam_w,
      jnp.tile(subln_g.reshape(1, 2 * D_HEAD), (1, H_B)), *([ck] * G), *([cv] * G))
    return out.reshape(bsz, C_B)


def _sb_dec_kernel(pt_ref, dn_ref, q_ref, acc0_ref, c0_ref, *rest, G):
    k_refs, v_refs = rest[:G], rest[G:2 * G]
    o_ref, acc_out_ref, c_out_ref, qm_ref, c_ref, acc_ref, done_ref = rest[2 * G:]
    b = pl.program_id(0)
    j = pl.program_id(1)
    shape = (DEC_ROWS, C_C)
    rowi = lax.broadcasted_iota(I32, shape, 0)
    lane = lax.broadcasted_iota(I32, shape, 1)
    sel = (lane // D_HEAD) == rowi

    @pl.when(j == 0)
    def _():
        qm_ref[...] = jnp.where(sel, q_ref[0] * (D_HEAD ** -0.5), 0.0).astype(BF16)
        c_ref[...] = c0_ref[0][:, 0:1]
        acc_ref[...] = acc0_ref[0]
        done_ref[0] = dn_ref[b]

    @pl.when(done_ref[0] == 0)
    def _():
        rr = lax.broadcasted_iota(I32, (PAGE_SIZE, PAGE_SIZE), 0)
        cc = lax.broadcasted_iota(I32, (PAGE_SIZE, PAGE_SIZE), 1)
        upper = jnp.where(rr > cc, 1.0, 0.0).astype(BF16)
        qm = qm_ref[...]
        c_run = c_ref[...]
        acc = acc_ref[...]
        for gi in range(G):
            z = _dot(qm, k_refs[gi][...].astype(BF16))
            logb, l1 = _log_sigmoid_pair(z)
            suf = _mm_exact_rhs(l1, upper, 2)
            att = jnp.exp(logb + suf + c_run)
            acc = acc + _dot(att.astype(BF16), v_refs[gi][...].astype(BF16), NT)
            c_run = c_run + jnp.sum(l1, axis=-1, keepdims=True)
        acc_ref[...] = acc
        c_ref[...] = c_run
        row1 = lax.broadcasted_iota(I32, (DEC_ROWS, 1), 0)
        cmax = jnp.max(jnp.where(row1 < H_C, c_run, NEG_BIG))
        done_ref[0] = (cmax < SB_DEAD).astype(I32)

    @pl.when(j == pl.num_programs(1) - 1)
    def _():
        acc = acc_ref[...]
        o_ref[0] = jnp.sum(jnp.where(sel, acc, 0.0), axis=0, keepdims=True)
        acc_out_ref[0] = acc
        c_out_ref[0] = jnp.broadcast_to(c_ref[...], (DEC_ROWS, LANES))


def _sb_dec_phase(q3, ck, cv, page_table, done, acc0, c0, layer, start, n_phase, G):
    bsz, n_pages = page_table.shape

    def page_spec(gi):
        def idx(b, j, pt, dn):
            page = pt[b, n_pages - 1 - (start + j * G + gi)]
            return (jnp.where(dn[b] == 0, page, pt[0, 0]), layer, 0, 0)
        return pl.BlockSpec((None, None, C_C, PAGE_SIZE), idx)

    row = lambda shape: pl.BlockSpec((1,) + shape, lambda b, j, pt, dn: (b, 0, 0))
    kern = functools.partial(_sb_dec_kernel, G=G)
    grid_spec = pltpu.PrefetchScalarGridSpec(
        num_scalar_prefetch=2,
        grid=(bsz, n_phase // G),
        in_specs=[row((1, C_C)), row((DEC_ROWS, C_C)), row((DEC_ROWS, LANES))]
                 + [page_spec(gi) for gi in range(G)] + [page_spec(gi) for gi in range(G)],
        out_specs=[row((1, C_C)), row((DEC_ROWS, C_C)), row((DEC_ROWS, LANES))],
        scratch_shapes=[pltpu.VMEM((DEC_ROWS, C_C), BF16), pltpu.VMEM((DEC_ROWS, 1), F32),
                        pltpu.VMEM((DEC_ROWS, C_C), F32), pltpu.SMEM((1,), I32)])
    return pl.pallas_call(
        kern, grid_spec=grid_spec,
        out_shape=[jax.ShapeDtypeStruct((bsz, 1, C_C), F32), jax.ShapeDtypeStruct((bsz, DEC_ROWS, C_C), F32),
                   jax.ShapeDtypeStruct((bsz, DEC_ROWS, LANES), F32)],
        compiler_params=_cp(("parallel", "arbitrary")),
        name="sb_dec",
    )(page_table, done, q3, acc0, c0, *([ck] * G), *([cv] * G))


SB_DEC_HEAD_PAGES = 8
SB_DEC_TAIL_GROUP = 12


def _sb_dec(qc, cache_k, cache_v, page_table, layer):
    bsz, n_pages = page_table.shape
    nl = cache_k.shape[1]
    ck = jnp.transpose(cache_k, (0, 1, 3, 4, 2)).reshape(cache_k.shape[0], nl, C_C, PAGE_SIZE)
    cv = jnp.transpose(cache_v, (0, 1, 3, 4, 2)).reshape(cache_v.shape[0], nl, C_C, PAGE_SIZE)
    q3 = qc.reshape(bsz, 1, C_C)
    n_head = min(SB_DEC_HEAD_PAGES, n_pages)
    acc0 = jnp.zeros((bsz, DEC_ROWS, C_C), F32)
    c0 = jnp.zeros((bsz, DEC_ROWS, LANES), F32)
    live = jnp.zeros((bsz,), I32)
    out, acc, c = _sb_dec_phase(q3, ck, cv, page_table, live, acc0, c0, layer, 0, n_head, n_head)
    n_tail = n_pages - n_head
    if n_tail:
        done = (jnp.max(c[:, :H_C, 0], axis=1) < SB_DEAD).astype(I32)
        out, _, _ = _sb_dec_phase(q3, ck, cv, page_table, done, acc, c, layer, n_head, n_tail,
                                  _pages_per_step(n_tail, SB_DEC_TAIL_GROUP))
    return out.reshape(bsz, C_C)


def _merge_kernel(h_ref, oa_ref, ob_ref, oc_ref, g0_ref, g1_ref, g2_ref, wa_ref, wb_ref, wc_ref, wo_ref,
                  gn_ref, rw_ref, rb_ref, h_out_ref, xn_ref, ridx_ref, rgate_ref):
    mix = (_sigmoid(g0_ref[...]) * _dot(oa_ref[...].astype(BF16), wa_ref[...])
           + _sigmoid(g1_ref[...]) * _dot(ob_ref[...].astype(BF16), wb_ref[...])
           + _sigmoid(g2_ref[...]) * _dot(oc_ref[...].astype(BF16), wc_ref[...]))
    h = h_ref[...] + _dot(mix.astype(BF16), wo_ref[...])
    h_out_ref[...] = h
    ms = jnp.mean(h * h, axis=-1, keepdims=True)
    xn = h * lax.rsqrt(ms + RMS_EPS) * gn_ref[...]
    for s in range(ROW_TILES):
        xn_ref[pl.ds(s, xn.shape[0], stride=ROW_TILES), :] = xn[:, s * LANES:(s + 1) * LANES]
    logits = _mm(xn, rw_ref[...], NN, 3) + rb_ref[...]
    tm = logits.shape[0]
    lane = lax.broadcasted_iota(I32, logits.shape, 1)
    lane_o = lax.broadcasted_iota(I32, (tm, LANES), 1)
    work = logits
    vals, idxs = [], []
    for _ in range(TOP_K):
        mx = jnp.max(work, axis=-1, keepdims=True)
        ik = jnp.min(jnp.where(work == mx, lane, N_EXPERTS), axis=-1, keepdims=True)
        vals.append(mx)
        idxs.append(ik)
        work = jnp.where(lane == ik, -jnp.inf, work)
    es = [jnp.exp(vk - vals[0]) for vk in vals]
    den = es[0] + es[1] + es[2] + es[3]
    ridx = jnp.zeros((tm, LANES), I32)
    rgate = jnp.zeros((tm, LANES), F32)
    for kx in range(TOP_K):
        ridx = jnp.where(lane_o == kx, idxs[kx], ridx)
        rgate = jnp.where(lane_o == kx, es[kx] / den, rgate)
    ridx_ref[...] = ridx
    rgate_ref[...] = rgate


def _merge(h, o_a, o_b, o_c, z, W):
    n, d = h.shape
    tm = min(512, n)
    gcol = ZG0 // d
    row = lambda c: pl.BlockSpec((tm, c), lambda i: (i, 0))
    full = lambda a: pl.BlockSpec(a.shape, lambda i: (0,) * a.ndim)
    wa, wb, wc, wo = (W['w_branch_a'].astype(BF16), W['w_branch_b'].astype(BF16),
                      W['w_branch_c'].astype(BF16), W['w_out'].astype(BF16))
    gn = W['norm_ffn_g'].reshape(1, d)
    rw = W['router_w']
    rb = W['router_b'].reshape(1, N_EXPERTS)
    return pl.pallas_call(
        _merge_kernel,
        grid=(n // tm,),
        in_specs=[row(d), row(C_A), row(C_B), row(C_C),
                  pl.BlockSpec((tm, d), lambda i: (i, gcol)),
                  pl.BlockSpec((tm, d), lambda i: (i, gcol + 1)),
                  pl.BlockSpec((tm, d), lambda i: (i, gcol + 2)),
                  full(wa), full(wb), full(wc), full(wo), full(gn), full(rw), full(rb)],
        out_specs=[row(d), pl.BlockSpec((tm * ROW_TILES, LANES), lambda i: (i, 0)), row(LANES), row(LANES)],
        out_shape=[jax.ShapeDtypeStruct((n, d), F32), jax.ShapeDtypeStruct((n * ROW_TILES, LANES), F32),
                   jax.ShapeDtypeStruct((n, LANES), I32), jax.ShapeDtypeStruct((n, LANES), F32)],
        compiler_params=_cp(("parallel",), VMEM_LIMIT),
        name="merge",
    )(h, o_a, o_b, o_c, z, z, z, wa, wb, wc, wo, gn, rw, rb)


def _route_kernel(ridx_ref, rpos_ref, cnt_ref, base_ref):
    i = pl.program_id(0)

    @pl.when(i == 0)
    def _():
        base_ref[...] = jnp.zeros(base_ref.shape, F32)

    ridx = ridx_ref[...]
    tm = ridx.shape[0]
    lane_e = lax.broadcasted_iota(I32, (tm, N_EXPERTS), 1)
    lane_o = lax.broadcasted_iota(I32, (tm, LANES), 1)
    onehots = [(lane_e == ridx[:, kx:kx + 1]).astype(F32) for kx in range(TOP_K)]
    sel = onehots[0] + onehots[1] + onehots[2] + onehots[3]
    rr = lax.broadcasted_iota(I32, (tm, tm), 0)
    cc = lax.broadcasted_iota(I32, (tm, tm), 1)
    before = jnp.where(cc < rr, 1.0, 0.0).astype(BF16)
    rank = _dot(before, sel.astype(BF16)) + base_ref[...]
    rpos = jnp.zeros((tm, LANES), I32)
    for kx in range(TOP_K):
        pk = jnp.sum(onehots[kx] * rank, axis=-1, keepdims=True)
        rpos = jnp.where(lane_o == kx, pk.astype(I32), rpos)
    rpos_ref[...] = rpos
    base_ref[...] = base_ref[...] + jnp.sum(sel, axis=0, keepdims=True)
    cnt_ref[...] = base_ref[...]


def _route(ridx):
    n = ridx.shape[0]
    tm = min(256, n)
    return pl.pallas_call(
        _route_kernel,
        grid=(n // tm,),
        in_specs=[pl.BlockSpec((tm, LANES), lambda i: (i, 0))],
        out_specs=[pl.BlockSpec((tm, LANES), lambda i: (i, 0)), pl.BlockSpec((1, N_EXPERTS), lambda i: (0, 0))],
        out_shape=[jax.ShapeDtypeStruct((n, LANES), I32), jax.ShapeDtypeStruct((1, N_EXPERTS), F32)],
        scratch_shapes=[pltpu.VMEM((1, N_EXPERTS), F32)],
        compiler_params=_cp(("arbitrary",)),
        name="route",
    )(ridx)


def _row_copy(src, src_row, dst, dst_row, sem):
    return pltpu.make_async_copy(src.at[pl.ds(pl.multiple_of(src_row * ROW_TILES, ROW_TILES), ROW_TILES)],
                                 dst.at[pl.ds(pl.multiple_of(dst_row * ROW_TILES, ROW_TILES), ROW_TILES)], sem)


def _dispatch_kernel(dest_ref, tail_ref, nused_ref, x_ref, xs_ref, zbuf_ref, sem_z, sem, *, tm, blk, n_blk):
    i = pl.program_id(0)

    def zero_copy(row0):
        return pltpu.make_async_copy(
            zbuf_ref, xs_ref.at[pl.ds(pl.multiple_of(row0 * ROW_TILES, ROW_TILES), blk * ROW_TILES)], sem_z)

    @pl.when(i == 0)
    def _():
        zbuf_ref[...] = jnp.zeros(zbuf_ref.shape, F32)
        for e in range(N_EXPERTS):
            pl.when(tail_ref[e] >= 0)(lambda e=e: zero_copy(tail_ref[e]).start())
        for e in range(N_EXPERTS):
            pl.when(tail_ref[e] >= 0)(lambda e=e: zero_copy(tail_ref[e]).wait())

        def fill(b, carry):
            cp = zero_copy(b * blk)
            cp.start()
            cp.wait()
            return carry

        lax.fori_loop(nused_ref[0], n_blk, fill, 0)

    def issue(r, carry):
        for kx in range(TOP_K):
            _row_copy(x_ref, r, xs_ref, dest_ref[r * TOP_K + kx], sem).start(priority=kx % 2)
        return carry

    lax.fori_loop(0, tm, issue, 0)

    def drain(r, carry):
        for kx in range(TOP_K):
            _row_copy(x_ref, 0, xs_ref, 0, sem).wait()
        return carry

    lax.fori_loop(0, tm, drain, 0)


def _dispatch(xn_tiles, dest_flat, tail_rows, nused, n_blk, blk):
    n = xn_tiles.shape[0] // ROW_TILES
    tm = min(256, n)
    kern = functools.partial(_dispatch_kernel, tm=tm, blk=blk, n_blk=n_blk)
    return pl.pallas_call(
        kern,
        grid=(n // tm,),
        in_specs=[pl.BlockSpec((tm * TOP_K,), lambda i: (i,), memory_space=pltpu.SMEM),
                  pl.BlockSpec(memory_space=pltpu.SMEM),
                  pl.BlockSpec(memory_space=pltpu.SMEM),
                  pl.BlockSpec((tm * ROW_TILES, LANES), lambda i: (i, 0))],
        out_specs=pl.BlockSpec(memory_space=pl.ANY),
        out_shape=jax.ShapeDtypeStruct((n_blk * blk * ROW_TILES, LANES), F32),
        scratch_shapes=[pltpu.VMEM((blk * ROW_TILES, LANES), F32),
                        pltpu.SemaphoreType.DMA(()), pltpu.SemaphoreType.DMA(())],
        compiler_params=_cp(("arbitrary",)),
        name="moe_dispatch",
    )(dest_flat, tail_rows, nused, xn_tiles)


def _rows_from_tiles(ref, n):
    return jnp.concatenate([ref[pl.ds(s, n, stride=ROW_TILES), :] for s in range(ROW_TILES)], axis=1)


def _expert_kernel(blk_e, nused, x_ref, wgu_ref, bgu_ref, wd_ref, bd_ref, y_ref, wgu_bf, wd_bf, *, blk, d_ff):
    i = pl.program_id(0)

    @pl.when(i < nused[0])
    def _():
        e = blk_e[i]
        e_prev = blk_e[jnp.maximum(i - 1, 0)]

        @pl.when(jnp.logical_or(i == 0, e != e_prev))
        def _():
            wgu_bf[...] = wgu_ref[...].astype(BF16)
            wd_bf[...] = wd_ref[...].astype(BF16)

        x = _rows_from_tiles(x_ref, blk).astype(BF16)
        hh = _dot(x, wgu_bf[...]) + bgu_ref[...]
        hg = jnp.minimum(hh[:, :d_ff], SWIGLU_LIMIT)
        hl = jnp.clip(hh[:, d_ff:], -SWIGLU_LIMIT, SWIGLU_LIMIT)
        act = hg * _sigmoid(SWIGLU_ALPHA * hg) * (hl + 1.0)
        y = _dot(act.astype(BF16), wd_bf[...]) + bd_ref[...]
        for s in range(ROW_TILES):
            y_ref[pl.ds(s, blk, stride=ROW_TILES), :] = y[:, s * LANES:(s + 1) * LANES]

    @pl.when(i >= nused[0])
    def _():
        y_ref[...] = jnp.zeros(y_ref.shape, F32)


def _experts(xs, blk_e, nused, w_gu, b_gu, w_down, b_down, layer, blk):
    n_blk = blk_e.shape[0]
    _, n_e, d, f2 = w_gu.shape
    d_ff = f2 // 2
    kern = functools.partial(_expert_kernel, blk=blk, d_ff=d_ff)
    last = lambda i, be, nu: jnp.minimum(i, nu[0] - 1)
    grid_spec = pltpu.PrefetchScalarGridSpec(
        num_scalar_prefetch=2,
        grid=(n_blk,),
        in_specs=[pl.BlockSpec((blk * ROW_TILES, LANES), lambda i, be, nu: (last(i, be, nu), 0)),
                  pl.BlockSpec((None, None, d, f2), lambda i, be, nu: (layer, be[i], 0, 0)),
                  pl.BlockSpec((None, 1, f2), lambda i, be, nu: (be[i], 0, 0)),
                  pl.BlockSpec((None, None, d_ff, d), lambda i, be, nu: (layer, be[i], 0, 0)),
                  pl.BlockSpec((None, 1, d), lambda i, be, nu: (be[i], 0, 0))],
        out_specs=pl.BlockSpec((blk * ROW_TILES, LANES), lambda i, be, nu: (i, 0)),
        scratch_shapes=[pltpu.VMEM((d, f2), BF16), pltpu.VMEM((d_ff, d), BF16)])
    return pl.pallas_call(
        kern, grid_spec=grid_spec,
        out_shape=jax.ShapeDtypeStruct(xs.shape, F32),
        compiler_params=_cp(("arbitrary",), VMEM_LIMIT),
        name="moe_experts",
    )(blk_e, nused, xs, w_gu, b_gu.reshape(n_e, 1, f2), w_down, b_down.reshape(n_e, 1, d))


def _combine_kernel(dest_ref, dest_next_ref, gate_ref, h_ref, pe_ref, yb_ref, gp_ref, pw_ref, pg_ref, o_ref,
                    buf_ref, sems, *, tm, nsteps):
    i = pl.program_id(0)
    slot = i % 2

    def gather(dref, sl):
        def issue(r, carry):
            for kx in range(TOP_K):
                _row_copy(yb_ref, dref[r * TOP_K + kx], buf_ref.at[sl, kx], r, sems.at[sl]).start(priority=kx % 2)
            return carry

        lax.fori_loop(0, tm, issue, 0)

    pl.when(i == 0)(lambda: gather(dest_ref, 0))
    pl.when(i + 1 < nsteps)(lambda: gather(dest_next_ref, 1 - slot))

    def drain(r, carry):
        for kx in range(TOP_K):
            _row_copy(yb_ref, 0, buf_ref.at[slot, kx], 0, sems.at[slot]).wait()
        return carry

    lax.fori_loop(0, tm, drain, 0)

    gate = gate_ref[...]
    h = h_ref[...]
    for kx in range(TOP_K):
        h = h + gate[:, kx:kx + 1] * _rows_from_tiles(buf_ref.at[slot, kx], tm)
    ms = jnp.mean(h * h, axis=-1, keepdims=True)
    xn = (h * lax.rsqrt(ms + RMS_EPS) * gp_ref[...]).astype(BF16)
    o_ref[...] = h + _dot(pe_ref[...].astype(BF16), pw_ref[...]) * _sigmoid(_dot(xn, pg_ref[...]))


def _combine(h, pe, yb, dest_flat, rgate, W):
    n, d = h.shape
    tm = min(256, n)
    gp = W['norm_ple_g'].reshape(1, d)
    pw = W['ple_w'].astype(BF16)
    pg = W['ple_gate_w'].astype(BF16)
    full = lambda a: pl.BlockSpec(a.shape, lambda i: (0,) * a.ndim)
    nsteps = n // tm
    kern = functools.partial(_combine_kernel, tm=tm, nsteps=nsteps)
    return pl.pallas_call(
        kern,
        grid=(nsteps,),
        in_specs=[pl.BlockSpec((tm * TOP_K,), lambda i: (i,), memory_space=pltpu.SMEM),
                  pl.BlockSpec((tm * TOP_K,), lambda i: (jnp.minimum(i + 1, nsteps - 1),), memory_space=pltpu.SMEM),
                  pl.BlockSpec((tm, LANES), lambda i: (i, 0)),
                  pl.BlockSpec((tm, d), lambda i: (i, 0)),
                  pl.BlockSpec((tm, pe.shape[1]), lambda i: (i, 0)),
                  pl.BlockSpec(memory_space=pl.ANY),
                  full(gp), full(pw), full(pg)],
        out_specs=pl.BlockSpec((tm, d), lambda i: (i, 0)),
        out_shape=jax.ShapeDtypeStruct((n, d), F32),
        scratch_shapes=[pltpu.VMEM((2, TOP_K, tm * ROW_TILES, LANES), F32), pltpu.SemaphoreType.DMA((2,))],
        compiler_params=_cp(("arbitrary",)),
        name="moe_combine",
    )(dest_flat, dest_flat, rgate, h, pe, yb, gp, pw, pg)


def _moe_and_ple(h, xn, ridx, rgate, pe, W):
    n, d = h.shape
    m = n * TOP_K
    blk = max(16, min(512, (m // N_EXPERTS) // 16 * 16))
    n_blk = -(-m // blk) + N_EXPERTS
    rpos, counts = _route(ridx)
    counts = counts[0].astype(I32)
    padded = (counts + blk - 1) // blk * blk
    pad_end = jnp.cumsum(padded)
    pad_start = pad_end - padded
    dest = pad_start[ridx[:, :TOP_K]] + rpos[:, :TOP_K]
    dest_flat = dest.reshape(m).astype(I32)
    blk_row0 = jnp.arange(n_blk, dtype=I32) * blk
    blk_e = jnp.minimum(jnp.sum((pad_end[None, :] <= blk_row0[:, None]).astype(I32), axis=1), N_EXPERTS - 1)
    nused = (pad_end[-1] // blk).astype(I32).reshape(1)
    blk_e = jnp.where(jnp.arange(n_blk) < nused[0], blk_e, blk_e[jnp.maximum(nused[0] - 1, 0)])
    tail_rows = jnp.where(counts % blk != 0, pad_end - blk, -1).astype(I32)
    xn_tiles = xn
    xs = _dispatch(xn_tiles, dest_flat, tail_rows, nused, n_blk, blk)
    yb = _experts(xs, blk_e, nused, W['moe_w_gu'], W['moe_b_gu'], W['moe_w_down'], W['moe_b_down'], W['layer'], blk)
    return _combine(h, pe, yb, dest_flat, rgate, W)


def _layer_common(h, z, o_a, o_b, o_c, pe, W):
    h_mid, xn, ridx, rgate = _merge(h, o_a, o_b, o_c, z, W)
    return _moe_and_ple(h_mid, xn, ridx, rgate, pe, W)


def _layer_prompt(h, pe, wp, W, bsz, t_len, lam_init):
    n = bsz * t_len
    z = _in_proj(h, W['norm_mix_g'], wp)
    L = min(64, t_len)
    shift0 = jnp.zeros((bsz, C_RWKV), F32)
    wkv0 = jnp.zeros((bsz, H_A, D_HEAD, D_HEAD), F32)
    o_a, wkv_new, shift_new = _rwkv(z, shift0, wkv0, W, bsz, t_len, L)
    qn, kn = _qknorm(z, W['diff_q_norm'], W['diff_k_norm'])
    o_b = _diff_attn(qn, kn, z, W['diff_lambda'], W['diff_subln_g'], bsz, t_len, lam_init)
    o_c = _sb_attn(z, bsz, t_len)
    h = _layer_common(h, z, o_a, o_b, o_c, pe, W)
    rows = (kn.reshape(bsz, t_len, H_B, 2 * D_HEAD),
            z[:, VB0:VB0 + C_B].reshape(bsz, t_len, H_B, 2 * D_HEAD),
            z[:, KC0:KC0 + C_C].reshape(bsz, t_len, H_C, D_HEAD),
            z[:, VC0:VC0 + C_C].reshape(bsz, t_len, H_C, D_HEAD),
            wkv_new, shift_new)
    return h, rows


DEC_CHUNK = 16


def _layer_sample(h, pe, wp, W, caches, page_table, shift0, wkv0, layer, lam_init):
    bsz = h.shape[0]
    z = _in_proj(h, W['norm_mix_g'], wp)
    za = jnp.pad(z[:, None, :ZA_W], ((0, 0), (0, DEC_CHUNK - 1), (0, 0))).reshape(bsz * DEC_CHUNK, ZA_W)
    o_a, wkv_new, shift_new = _rwkv(za, shift0, wkv0, W, bsz, 1, DEC_CHUNK)
    o_a = o_a.reshape(bsz, DEC_CHUNK, C_A)[:, 0]
    qn, kn = _qknorm(z, W['diff_q_norm'], W['diff_k_norm'])
    vb = z[:, VB0:VB0 + C_B]
    o_b = _diff_dec(qn, kn, vb, caches[0], caches[1], page_table, layer, W['diff_lambda'], W['diff_subln_g'], lam_init)
    o_c = _sb_dec(z[:, QC0:QC0 + C_C], caches[2], caches[3], page_table, layer)
    h = _layer_common(h, z, o_a, o_b, o_c, pe, W)
    rows = (kn.reshape(bsz, 1, H_B, 2 * D_HEAD), vb.reshape(bsz, 1, H_B, 2 * D_HEAD),
            z[:, KC0:KC0 + C_C].reshape(bsz, 1, H_C, D_HEAD), z[:, VC0:VC0 + C_C].reshape(bsz, 1, H_C, D_HEAD),
            wkv_new, shift_new)
    return h, rows


def kernel(x_prompt, x_sample, cache_diff_k, cache_diff_v, cache_sb_k, cache_sb_v, state_wkv, state_shift, page_table, p_prompt, p_sample, norm_mix_g, w_in, rwkv_mu, rwkv_w0, rwkv_w2, rwkv_a0, rwkv_a2, rwkv_g2, rwkv_k_k, rwkv_k_a, rwkv_r_k, rwkv_lnx_g, rwkv_lnx_b, diff_q_norm, diff_k_norm, diff_lambda, diff_subln_g, w_branch_a, w_branch_b, w_branch_c, w_out, norm_ffn_g, router_w, router_b, moe_w_gu, moe_b_gu, moe_w_down, moe_b_down, norm_ple_g, ple_w, ple_gate_w):
    bsz_p, seq_p, d = x_prompt.shape
    bsz_s, seq_s, _ = x_sample.shape
    assert seq_s == 1 and d == ROW_TILES * LANES
    depth = w_in.shape[0]
    params = dict(norm_mix_g=norm_mix_g, rwkv_mu=rwkv_mu, rwkv_w0=rwkv_w0, rwkv_w2=rwkv_w2, rwkv_a0=rwkv_a0,
                  rwkv_a2=rwkv_a2, rwkv_g2=rwkv_g2, rwkv_k_k=rwkv_k_k, rwkv_k_a=rwkv_k_a, rwkv_r_k=rwkv_r_k,
                  rwkv_lnx_g=rwkv_lnx_g, rwkv_lnx_b=rwkv_lnx_b, diff_q_norm=diff_q_norm, diff_k_norm=diff_k_norm,
                  diff_lambda=diff_lambda, diff_subln_g=diff_subln_g, w_branch_a=w_branch_a,
                  w_branch_b=w_branch_b, w_branch_c=w_branch_c, w_out=w_out, norm_ffn_g=norm_ffn_g,
                  router_w=router_w, router_b=router_b, moe_w_gu=moe_w_gu, moe_b_gu=moe_b_gu,
                  moe_w_down=moe_w_down, moe_b_down=moe_b_down, norm_ple_g=norm_ple_g, ple_w=ple_w,
                  ple_gate_w=ple_gate_w)
    caches = (cache_diff_k, cache_diff_v, cache_sb_k, cache_sb_v)
    h_p = x_prompt.reshape(bsz_p * seq_p, d)
    h_s = x_sample.reshape(bsz_s, d)
    rows_p, rows_s = [], []
    for i in range(depth):
        W = {name: (val if name in ('moe_w_gu', 'moe_w_down') else val[i]) for name, val in params.items()}
        W['layer'] = i
        wp = (w_in, i)
        lam_init = 0.8 - 0.6 * math.exp(-0.3 * i)
        h_p, r_p = _layer_prompt(h_p, p_prompt[i].reshape(bsz_p * seq_p, -1), wp, W, bsz_p, seq_p, lam_init)
        h_s, r_s = _layer_sample(h_s, p_sample[i].reshape(bsz_s, -1), wp, W, caches, page_table,
                                 state_shift[:, i], state_wkv[:, i], i, lam_init)
        rows_p.append(r_p)
        rows_s.append(r_s)
    st = lambda rows, j: jnp.stack([r[j] for r in rows], axis=1)
    return (h_p.reshape(bsz_p, seq_p, d), h_s.reshape(bsz_s, seq_s, d),
            st(rows_p, 0), st(rows_p, 1), st(rows_p, 2), st(rows_p, 3), st(rows_p, 4), st(rows_p, 5),
            st(rows_s, 0), st(rows_s, 1), st(rows_s, 2), st(rows_s, 3), st(rows_s, 4), st(rows_s, 5))
```

```python
import functools
import math

import numpy as np
import jax
import jax.numpy as jnp
from jax import lax
from jax.experimental import pallas as pl
from jax.experimental.pallas import tpu as pltpu

F32 = jnp.float32
BF16 = jnp.bfloat16
I32 = jnp.int32

D_HEAD = 64
H_A, H_B, H_C = 8, 4, 6
C_A, C_B, C_C = H_A * D_HEAD, H_B * 2 * D_HEAD, H_C * D_HEAD
D_DECAY_LORA, D_AAA_LORA, D_GATE_LORA = 64, 64, 128
C_RWKV = 3 * C_A + D_DECAY_LORA + D_AAA_LORA + D_GATE_LORA
N_EXPERTS, TOP_K = 32, 4
SWIGLU_LIMIT, SWIGLU_ALPHA = 7.0, 1.702
PAGE_SIZE = 128
RMS_EPS = 1e-6
GN_EPS = 64e-5

LANES = 128
SUBLANES = 8
ROW_TILES = 8

ZA0, ZA_W = 0, 2048
QB0, KB0, VB0 = 2048, 2560, 3072
QC0, KC0, VC0 = 3584, 4096, 4608
ZG0 = 5120
ZW = 8192

SB_DEAD = -110.0
NEG_BIG = -1e30

NN = (((1,), (0,)), ((), ()))
NT = (((1,), (1,)), ((), ()))
TN = (((0,), (0,)), ((), ()))

VMEM_LIMIT = 56 * 1024 * 1024


def _cp(sem, vmem=None):
    return pltpu.CompilerParams(dimension_semantics=sem, vmem_limit_bytes=vmem)


def _dot(a, b, dims=NN):
    return lax.dot_general(a, b, dims, preferred_element_type=F32)


def _split(x):
    hi = x.astype(BF16)
    lo = (x - hi.astype(F32)).astype(BF16)
    return hi, lo


def _mm(a, b, dims=NN, passes=1):
    if passes == 1:
        return _dot(a.astype(BF16), b.astype(BF16), dims)
    ah, al = _split(a)
    bh, bl = _split(b)
    return _dot(ah, bh, dims) + (_dot(ah, bl, dims) + _dot(al, bh, dims))


def _mm_exact_rhs(a, b_exact, passes=2):
    hi = a.astype(BF16)
    out = _dot(hi, b_exact)
    rem = a - hi.astype(F32)
    for _ in range(passes - 1):
        part = rem.astype(BF16)
        out = out + _dot(part, b_exact)
        rem = rem - part.astype(F32)
    return out


def _mm_exact_lhs(a_exact, b, passes=2):
    hi = b.astype(BF16)
    out = _dot(a_exact, hi)
    rem = b - hi.astype(F32)
    for _ in range(passes - 1):
        part = rem.astype(BF16)
        out = out + _dot(a_exact, part)
        rem = rem - part.astype(F32)
    return out


def _sigmoid(x):
    return 1.0 / (1.0 + jnp.exp(-x))


def _log_sigmoid_pair(z):
    lg = jnp.log(1.0 + jnp.exp(-jnp.abs(z)))
    return jnp.minimum(z, 0.0) - lg, -jnp.maximum(z, 0.0) - lg


def _block_indicator(n, blk):
    idx = np.arange(n) // blk
    return jnp.asarray((idx[:, None] == idx[None, :]).astype(np.float32), dtype=BF16)


IN_PROJ_TN = 512
IN_PROJ_SUB = IN_PROJ_TN // LANES


def _in_proj_kernel(src_ref, x_ref, g_ref, *rest):
    w_refs, (o_ref, xn_ref) = rest[:IN_PROJ_SUB], rest[IN_PROJ_SUB:]

    @pl.when(pl.program_id(1) == 0)
    def _():
        x = x_ref[...]
        ms = jnp.mean(x * x, axis=-1, keepdims=True)
        xn_ref[...] = (x * lax.rsqrt(ms + RMS_EPS) * g_ref[...]).astype(BF16)

    w = jnp.concatenate([w_ref[...].astype(BF16) for w_ref in w_refs], axis=1)
    o_ref[...] = _dot(xn_ref[...], w)


def _in_proj_columns():
    src = []

    def slot(first, count, width):
        blocks = list(range(first, first + count))
        src.extend(blocks + [blocks[-1]] * (width - count))

    c_diff = 3 * C_B
    slot(0, C_RWKV // LANES, ZA_W // LANES)
    slot(C_RWKV // LANES, c_diff // LANES, c_diff // LANES)
    for part in range(3):
        slot((C_RWKV + c_diff + part * C_C) // LANES, C_C // LANES, 512 // LANES)
    gate0 = (C_RWKV + c_diff + 3 * C_C) // LANES
    slot(gate0, (ZW - ZG0) // LANES, (ZW - ZG0) // LANES)
    assert len(src) == ZW // LANES
    return jnp.asarray(src, I32)


def _in_proj(x, g, w_in):
    n, d = x.shape
    tm = min(1024, n)
    w_all, layer = w_in
    w_spec = lambda k: pl.BlockSpec((None, d, LANES), lambda i, j, src: (layer, 0, src[j * IN_PROJ_SUB + k]))
    grid_spec = pltpu.PrefetchScalarGridSpec(
        num_scalar_prefetch=1,
        grid=(n // tm, ZW // IN_PROJ_TN),
        in_specs=[pl.BlockSpec((tm, d), lambda i, j, src: (i, 0)),
                  pl.BlockSpec((1, d), lambda i, j, src: (0, 0))] + [w_spec(k) for k in range(IN_PROJ_SUB)],
        out_specs=pl.BlockSpec((tm, IN_PROJ_TN), lambda i, j, src: (i, j)),
        scratch_shapes=[pltpu.VMEM((tm, d), BF16)])
    return pl.pallas_call(
        _in_proj_kernel, grid_spec=grid_spec,
        out_shape=jax.ShapeDtypeStruct((n, ZW), F32),
        compiler_params=_cp(("parallel", "arbitrary")),
        name="in_proj",
    )(_in_proj_columns(), x, g.reshape(1, d), *([w_all] * IN_PROJ_SUB))


RWKV_GROUP = 4
RWKV_PASSES = dict(tinv=1, pu=1, out=1, state=1)
RWKV_SEQS_PER_STEP = 2


def _bd(y, mask):
    yb = y.astype(BF16)
    return jnp.concatenate([yb] * RWKV_GROUP, axis=0) * mask


def _mm_bd(x, y, mask, passes, dims=NN):
    if passes == 1:
        return _dot(x.astype(BF16), _bd(y, mask), dims)
    xh, xl = _split(x)
    yh, yl = _split(y)
    bh, bl = _bd(yh, mask), _bd(yl, mask)
    return _dot(xh, bh, dims) + (_dot(xh, bl, dims) + _dot(xl, bh, dims))


def _tri_inv_cat(a, n, eye, same_blk, mask, passes):
    bs = min(16, n)
    if n > bs:
        ad = jnp.where(same_blk, a, 0.0)
        ao = a - ad
    else:
        ad, ao = a, None
    td = eye + ad
    pw = ad
    p = 2
    while p < bs:
        pw = _mm_bd(pw, pw, mask, passes)
        td = td + _mm_bd(td, pw, mask, passes)
        p *= 2
    if ao is None:
        return td
    nmat = _mm_bd(td, ao, mask, passes)
    res = eye + nmat
    npw = nmat
    p = 2
    while p < n // bs:
        npw = _mm_bd(npw, npw, mask, passes)
        res = res + _mm_bd(res, npw, mask, passes)
        p *= 2
    return _mm_bd(res, td, mask, passes)


def _rwkv_seq(bi, c, z_ref, mu_ref, vec_ref, w2_ref, a2_ref, g2_ref, bd_ref, mch_ref, mll_ref,
              o_ref, carry_ref, state_ref, *, L, nc, t_real):
    z = z_ref[bi]
    row1 = lax.broadcasted_iota(I32, (L, 1), 0)
    zp = jnp.where(row1 == 0, carry_ref[bi], pltpu.roll(z, 1, axis=0))
    carry_ref[bi] = z[L - 1:L, :]
    zs = z + (zp - z) * mu_ref[...]

    r = zs[:, 0:C_A]
    k = zs[:, C_A:2 * C_A]
    v = zs[:, 2 * C_A:3 * C_A]
    o1 = 3 * C_A
    w_lo = zs[:, o1:o1 + D_DECAY_LORA]
    a_lo = zs[:, o1 + D_DECAY_LORA:o1 + D_DECAY_LORA + D_AAA_LORA]
    g_lo = zs[:, o1 + D_DECAY_LORA + D_AAA_LORA:C_RWKV]

    w0, a0 = vec_ref[0:1, :], vec_ref[1:2, :]
    k_k, k_a, r_k = vec_ref[2:3, :], vec_ref[3:4, :], vec_ref[4:5, :]
    lnx_g, lnx_b = vec_ref[5:6, :], vec_ref[6:7, :]
    bd = bd_ref[...]

    xw = w0 + _mm(jnp.tanh(w_lo), w2_ref[...])
    w_log = -(jnp.maximum(-xw, 0.0) + jnp.log(1.0 + jnp.exp(-jnp.abs(xw)))) - 0.5
    logw = -jnp.exp(w_log)
    a = _sigmoid(a0 + _mm(a_lo, a2_ref[...]))
    g = _mm(_sigmoid(g_lo), g2_ref[...])
    kk = k * k_k
    ss = _mm_exact_rhs(kk * kk, bd, 2)
    kk = kk / jnp.maximum(jnp.sqrt(ss), 1e-12)
    k2 = k * (1.0 + (a - 1.0) * k_a)
    if nc * L != t_real:
        valid = (c * L + row1) < t_real
        logw = jnp.where(valid, logw, 0.0)
        kk = jnp.where(valid, kk, 0.0)
        k2 = jnp.where(valid, k2, 0.0)
        v = jnp.where(valid, v, 0.0)

    row = lax.broadcasted_iota(I32, (L, L), 0)
    col = lax.broadcasted_iota(I32, (L, L), 1)
    cs = _mm_exact_lhs(jnp.where(col <= row, 1.0, 0.0).astype(BF16), logw, 3)
    cs_l = cs[L - 1:L, :]
    e_in = jnp.exp(cs)
    e_ex = jnp.exp(cs - logw)
    e_neg = jnp.exp(-cs)
    e_end = jnp.exp(cs_l - cs)
    w_l = e_in[L - 1:L, :]
    kka = kk * a
    at = -kk * e_ex
    rt = r * e_in
    bt = kka * e_neg
    kt = k2 * e_neg
    bh = kka * e_end
    kh = k2 * e_end

    rk_sum = _mm_exact_rhs(r * k2 * r_k, bd, 2)
    bonus = rk_sum * v

    gw = RWKV_GROUP * D_HEAD
    gl = RWKV_GROUP * L
    t_i = lax.broadcasted_iota(I32, (L, gl), 0)
    j_i = lax.broadcasted_iota(I32, (L, gl), 1) % L
    tril_incl = j_i <= t_i
    tril_strict = j_i < t_i
    eye_cat = jnp.where(j_i == t_i, 1.0, 0.0).astype(F32)
    same_blk = (t_i // 16) == (j_i // 16)
    mch = mch_ref[...]
    mll = mll_ref[...]
    bdf = bd[:gw, :gw].astype(F32)
    diag = lax.broadcasted_iota(I32, (gw, gw), 0) == lax.broadcasted_iota(I32, (gw, gw), 1)
    pp = RWKV_PASSES

    ys = []
    for gi in range(H_A // RWKV_GROUP):
        sl = slice(gi * gw, (gi + 1) * gw)
        at_g, rt_g, bt_g, kt_g, bh_g, kh_g, v_g = at[:, sl], rt[:, sl], bt[:, sl], kt[:, sl], bh[:, sl], kh[:, sl], v[:, sl]
        ar = jnp.concatenate([at_g, rt_g], axis=0).astype(BF16)
        g_b = _dot(ar, _bd(bt_g, mch), NT)
        g_k = _dot(ar, _bd(kt_g, mch), NT)
        a_ab = jnp.where(tril_strict, g_b[:L], 0.0)
        a_rb = jnp.where(tril_incl, g_b[L:], 0.0)
        a_ak = jnp.where(tril_strict, g_k[:L], 0.0)
        a_rk = jnp.where(tril_incl, g_k[L:], 0.0)
        tinv = _tri_inv_cat(a_ab, L, eye_cat, same_blk, mll, pp['tinv'])
        v_bd = _bd(v_g, mch)
        av = _dot(a_ak.astype(BF16), v_bd)
        p_m = _mm_bd(tinv, at_g, mch, pp['pu'])
        u0 = _mm_bd(tinv, av, mch, pp['pu'])
        y0 = _dot(a_rk.astype(BF16), v_bd) + _mm_bd(a_rb, u0, mch, 1)
        q_m = rt_g + _mm_bd(a_rb, p_m, mch, 1)
        s_bd = state_ref[bi, gi]
        ys.append(y0 + _mm(q_m, s_bd, NN, pp['out']))
        m_t = _mm(bh_g, p_m, TN, pp['state']) * bdf + jnp.where(diag, w_l[:, sl], 0.0)
        c_t = _mm(jnp.concatenate([bh_g, kh_g], axis=0), jnp.concatenate([u0, v_g], axis=0), TN, pp['state']) * bdf
        state_ref[bi, gi] = _mm(m_t, s_bd, NN, pp['state']) + c_t
    y = jnp.concatenate(ys, axis=1)

    mean = _mm_exact_rhs(y, bd, 2) * (1.0 / D_HEAD)
    dlt = y - mean
    var = _mm_exact_rhs(dlt * dlt, bd, 2) * (1.0 / D_HEAD)
    yn = dlt * lax.rsqrt(var + GN_EPS)
    o_ref[bi] = (yn * lnx_g + lnx_b + bonus) * g


def _rwkv_kernel(z_ref, shift0_ref, sbd0_ref, mu_ref, vec_ref, w2_ref, a2_ref, g2_ref, bd_ref, mch_ref, mll_ref,
                 o_ref, sbd_out_ref, shift_out_ref, carry_ref, state_ref, *, L, nc, t_real, bb):
    c = pl.program_id(1)

    @pl.when(c == 0)
    def _():
        carry_ref[...] = shift0_ref[...]
        state_ref[...] = sbd0_ref[...]

    for bi in range(bb):
        _rwkv_seq(bi, c, z_ref, mu_ref, vec_ref, w2_ref, a2_ref, g2_ref, bd_ref, mch_ref, mll_ref,
                  o_ref, carry_ref, state_ref, L=L, nc=nc, t_real=t_real)

    @pl.when(c == nc - 1)
    def _():
        sbd_out_ref[...] = state_ref[...]
        rl = t_real - 1 - (nc - 1) * L
        shift_out_ref[...] = z_ref[:, rl:rl + 1, :]


def _rwkv(z_rows, shift0, wkv0, W, bsz, t_real, L):
    nc = -(-t_real // L)
    pad = lambda a: jnp.pad(a, ((0, 0), (0, ZA_W - a.shape[-1])))
    mu = pad(W['rwkv_mu'].reshape(1, C_RWKV))
    vec = jnp.concatenate([W['rwkv_w0'][None], W['rwkv_a0'][None], W['rwkv_k_k'][None], W['rwkv_k_a'][None],
                           W['rwkv_r_k'].reshape(1, C_A), W['rwkv_lnx_g'][None], W['rwkv_lnx_b'][None],
                           jnp.zeros((1, C_A), F32)], axis=0)
    shift0p = pad(shift0).reshape(bsz, 1, ZA_W)
    hg, ng = RWKV_GROUP, H_A // RWKV_GROUP
    gw, gl = hg * D_HEAD, hg * L
    eye_h = jnp.eye(hg, dtype=F32)
    st = jnp.swapaxes(wkv0, -1, -2).reshape(bsz, ng, hg, D_HEAD, D_HEAD)
    sbd0 = (st[:, :, :, :, None, :] * eye_h[None, None, :, None, :, None]).reshape(bsz, ng, gw, gw)
    rblk = np.arange(gl) // L
    mch = jnp.asarray((rblk[:, None] == (np.arange(gw) // D_HEAD)[None, :]).astype(np.float32), dtype=BF16)
    mll = jnp.asarray((rblk[:, None] == rblk[None, :]).astype(np.float32), dtype=BF16)
    bb = RWKV_SEQS_PER_STEP if bsz % RWKV_SEQS_PER_STEP == 0 else 1
    kern = functools.partial(_rwkv_kernel, L=L, nc=nc, t_real=t_real, bb=bb)
    full = lambda shape: pl.BlockSpec(shape, lambda b, c: (0,) * len(shape))
    o_a, sbd_new, shift_new = pl.pallas_call(
        kern,
        grid=(bsz // bb, nc),
        in_specs=[pl.BlockSpec((bb, L, ZA_W), lambda b, c: (b, c, 0)),
                  pl.BlockSpec((bb, 1, ZA_W), lambda b, c: (b, 0, 0)),
                  pl.BlockSpec((bb, ng, gw, gw), lambda b, c: (b, 0, 0, 0)),
                  full((1, ZA_W)), full((8, C_A)), full((D_DECAY_LORA, C_A)), full((D_AAA_LORA, C_A)),
                  full((D_GATE_LORA, C_A)), full((C_A, C_A)), full((gl, gw)), full((gl, gl))],
        out_specs=[pl.BlockSpec((bb, L, C_A), lambda b, c: (b, c, 0)),
                   pl.BlockSpec((bb, ng, gw, gw), lambda b, c: (b, 0, 0, 0)),
                   pl.BlockSpec((bb, 1, ZA_W), lambda b, c: (b, 0, 0))],
        out_shape=[jax.ShapeDtypeStruct((bsz, nc * L, C_A), F32),
                   jax.ShapeDtypeStruct((bsz, ng, gw, gw), F32),
                   jax.ShapeDtypeStruct((bsz, 1, ZA_W), F32)],
        scratch_shapes=[pltpu.VMEM((bb, 1, ZA_W), F32), pltpu.VMEM((bb, ng, gw, gw), F32)],
        compiler_params=_cp(("parallel", "arbitrary"), VMEM_LIMIT),
        name="rwkv",
    )(z_rows.reshape(bsz, nc * L, z_rows.shape[-1]), shift0p, sbd0, mu, vec, W['rwkv_w2'].astype(BF16),
      W['rwkv_a2'].astype(BF16), W['rwkv_g2'].astype(BF16), _block_indicator(C_A, D_HEAD), mch, mll)
    wkv_new = jnp.einsum('bghchv->bghvc', sbd_new.reshape(bsz, ng, hg, D_HEAD, hg, D_HEAD))
    wkv_new = wkv_new.reshape(bsz, H_A, D_HEAD, D_HEAD)
    return o_a.reshape(bsz * nc * L, C_A), wkv_new, shift_new[:, 0, :C_RWKV]


def _qknorm_kernel(q_ref, k_ref, gq_ref, gk_ref, bd_ref, qn_ref, kn_ref):
    bd = bd_ref[...]
    q = q_ref[...]
    k = k_ref[...]
    msq = _mm_exact_rhs(q * q, bd, 2) * (1.0 / D_HEAD)
    msk = _mm_exact_rhs(k * k, bd, 2) * (1.0 / D_HEAD)
    qn_ref[...] = q * lax.rsqrt(msq + RMS_EPS) * gq_ref[...] * (D_HEAD ** -0.5)
    kn_ref[...] = k * lax.rsqrt(msk + RMS_EPS) * gk_ref[...]


def _qknorm(z, gq, gk):
    n = z.shape[0]
    tm = min(512, n)
    tile = lambda g: jnp.tile(g.reshape(1, D_HEAD), (1, C_B // D_HEAD))
    return pl.pallas_call(
        _qknorm_kernel,
        grid=(n // tm,),
        in_specs=[pl.BlockSpec((tm, C_B), lambda i: (i, QB0 // C_B)),
                  pl.BlockSpec((tm, C_B), lambda i: (i, KB0 // C_B)),
                  pl.BlockSpec((1, C_B), lambda i: (0, 0)),
                  pl.BlockSpec((1, C_B), lambda i: (0, 0)),
                  pl.BlockSpec((C_B, C_B), lambda i: (0, 0))],
        out_specs=[pl.BlockSpec((tm, C_B), lambda i: (i, 0)), pl.BlockSpec((tm, C_B), lambda i: (i, 0))],
        out_shape=[jax.ShapeDtypeStruct((n, C_B), F32), jax.ShapeDtypeStruct((n, C_B), F32)],
        compiler_params=_cp(("parallel",)),
        name="qknorm",
    )(z, z, tile(gq), tile(gk), _block_indicator(C_B, D_HEAD))


def _lam_value(lam_ref, lam_init):
    lv = lam_ref[...]
    l1 = jnp.exp(jnp.sum(lv[0:1, :] * lv[1:2, :], axis=-1, keepdims=True))
    l2 = jnp.exp(jnp.sum(lv[2:3, :] * lv[3:4, :], axis=-1, keepdims=True))
    return l1 - l2 + lam_init


def _diff_attn_kernel(qi_tab, kj_tab, q_ref, k_ref, v_ref, lam_ref, g_ref, o_ref,
                      qs_ref, m_ref, l_ref, acc_ref, *, tq, tk, rg, lam_init):
    p = pl.program_id(2)
    qi = qi_tab[p]
    kj = kj_tab[p]

    @pl.when(kj == 0)
    def _():
        q = q_ref[...]
        lane = lax.broadcasted_iota(I32, q.shape, 1)
        qs_ref[0:tq, :] = jnp.where(lane < D_HEAD, q, 0.0).astype(BF16)
        qs_ref[tq:2 * tq, :] = jnp.where(lane >= D_HEAD, q, 0.0).astype(BF16)
        m_ref[...] = jnp.full(m_ref.shape, NEG_BIG, F32)
        l_ref[...] = jnp.zeros(l_ref.shape, F32)
        acc_ref[...] = jnp.zeros(acc_ref.shape, F32)

    def body(masked):
        k = k_ref[...].astype(BF16)
        v = v_ref[...].astype(BF16)
        for g in range(2 * tq // rg):
            cs = pl.ds(g * rg, rg)
            s = _dot(k, qs_ref[cs, :], NT)
            if masked:
                kpos = kj * tk + lax.broadcasted_iota(I32, (tk, rg), 0)
                qpos = qi * tq + (g * rg) % tq + lax.broadcasted_iota(I32, (tk, rg), 1)
                s = jnp.where(kpos <= qpos, s, NEG_BIG)
            m_prev = m_ref[:, cs]
            m_new = jnp.maximum(m_prev, jnp.max(s, axis=0, keepdims=True))
            alpha = jnp.exp(m_prev - m_new)
            pr = jnp.exp(s - m_new)
            l_ref[:, cs] = alpha * l_ref[:, cs] + jnp.sum(pr, axis=0, keepdims=True)
            acc_ref[:, cs] = alpha * acc_ref[:, cs] + _dot(v, pr.astype(BF16), TN)
            m_ref[:, cs] = m_new

    crosses = (kj + 1) * tk - 1 > qi * tq
    pl.when(crosses)(lambda: body(True))
    pl.when(jnp.logical_not(crosses))(lambda: body(False))

    @pl.when(kj == ((qi + 1) * tq - 1) // tk)
    def _():
        lam = _lam_value(lam_ref, lam_init)
        acc = acc_ref[...]
        inv_l = 1.0 / l_ref[...]
        o = acc[:, :tq] * inv_l[:, :tq] - lam * (acc[:, tq:] * inv_l[:, tq:])
        ms = jnp.mean(o * o, axis=0, keepdims=True)
        o = o * lax.rsqrt(ms + RMS_EPS) * g_ref[...] * (1.0 - lam_init)
        o_ref[...] = o.T


DIFF_ROW_GROUP = 128


def _causal_pairs(nq, tq, tk, descending):
    qi_l, kj_l = [], []
    for qi in range(nq):
        last = ((qi + 1) * tq - 1) // tk
        ks = range(last, -1, -1) if descending else range(last + 1)
        for kj in ks:
            qi_l.append(qi)
            kj_l.append(kj)
    return jnp.asarray(qi_l, I32), jnp.asarray(kj_l, I32)


def _diff_attn(qn, kn, z, lam_w, subln_g, bsz, t_len, lam_init):
    tq = tk = min(512, t_len)
    nq, nk = t_len // tq, t_len // tk
    qi_tab, kj_tab = _causal_pairs(nq, tq, tk, False)
    vcol = VB0 // LANES
    kern = functools.partial(_diff_attn_kernel, tq=tq, tk=tk, rg=min(DIFF_ROW_GROUP, tq), lam_init=lam_init)
    grid_spec = pltpu.PrefetchScalarGridSpec(
        num_scalar_prefetch=2,
        grid=(bsz, H_B, int(qi_tab.shape[0])),
        in_specs=[pl.BlockSpec((tq, LANES), lambda b, h, p, qt, kt: (b * nq + qt[p], h)),
                  pl.BlockSpec((tk, LANES), lambda b, h, p, qt, kt: (b * nk + kt[p], h)),
                  pl.BlockSpec((tk, LANES), lambda b, h, p, qt, kt: (b * nk + kt[p], vcol + h)),
                  pl.BlockSpec((4, D_HEAD), lambda b, h, p, qt, kt: (0, 0)),
                  pl.BlockSpec((LANES, 1), lambda b, h, p, qt, kt: (0, 0))],
        out_specs=pl.BlockSpec((tq, LANES), lambda b, h, p, qt, kt: (b * nq + qt[p], h)),
        scratch_shapes=[pltpu.VMEM((2 * tq, LANES), BF16), pltpu.VMEM((1, 2 * tq), F32),
                        pltpu.VMEM((1, 2 * tq), F32), pltpu.VMEM((LANES, 2 * tq), F32)])
    return pl.pallas_call(
        kern, grid_spec=grid_spec,
        out_shape=jax.ShapeDtypeStruct((bsz * t_len, C_B), F32),
        compiler_params=_cp(("parallel", "parallel", "arbitrary")),
        name="diff_attn",
    )(qi_tab, kj_tab, qn, kn, z, lam_w, subln_g.reshape(2 * D_HEAD, 1))


def _sb_attn_kernel(qi_tab, kj_tab, q_ref, k_ref, v_ref, o_ref, qs_ref, c_ref, acc_ref, done_ref,
                    *, tq, tkb, sub):
    p = pl.program_id(2)
    qi = qi_tab[p]
    kj = kj_tab[p]
    first_kj = ((qi + 1) * tq - 1) // tkb

    @pl.when(kj == first_kj)
    def _():
        q = q_ref[...] * (D_HEAD ** -0.5)
        lane = lax.broadcasted_iota(I32, q.shape, 1)
        qs_ref[0:tq, :] = jnp.where(lane < D_HEAD, q, 0.0).astype(BF16)
        qs_ref[tq:2 * tq, :] = jnp.where(lane >= D_HEAD, q, 0.0).astype(BF16)
        c_ref[...] = jnp.zeros(c_ref.shape, F32)
        acc_ref[...] = jnp.zeros(acc_ref.shape, F32)
        done_ref[0] = 0

    rr = lax.broadcasted_iota(I32, (sub, sub), 0)
    cc = lax.broadcasted_iota(I32, (sub, sub), 1)
    upper = jnp.where(rr > cc, 1.0, 0.0).astype(BF16)

    def body(masked):
        qs = qs_ref[...]
        for sb in reversed(range(tkb // sub)):
            ksl = slice(sb * sub, (sb + 1) * sub)
            z = _dot(qs, k_ref[ksl, :].astype(BF16), NT)
            logb, l1 = _log_sigmoid_pair(z)
            if masked:
                qpos = qi * tq + lax.broadcasted_iota(I32, (tq, sub), 0)
                kpos = kj * tkb + sb * sub + lax.broadcasted_iota(I32, (tq, sub), 1)
                ok = kpos < qpos
                ok = jnp.concatenate([ok, ok], axis=0)
                l1 = jnp.where(ok, l1, 0.0)
            suf = _mm_exact_rhs(l1, upper, 2)
            c_prev = c_ref[...]
            att = jnp.exp(logb + suf + c_prev)
            if masked:
                att = jnp.where(ok, att, 0.0)
            acc_ref[...] += _dot(att.astype(BF16), v_ref[ksl, :].astype(BF16))
            c_ref[...] = c_prev + jnp.sum(l1, axis=-1, keepdims=True)
        done_ref[0] = (jnp.max(c_ref[...]) < SB_DEAD).astype(I32)

    live = done_ref[0] == 0
    crosses = (kj + 1) * tkb > qi * tq
    pl.when(jnp.logical_and(live, crosses))(lambda: body(True))
    pl.when(jnp.logical_and(live, jnp.logical_not(crosses)))(lambda: body(False))

    @pl.when(kj == 0)
    def _():
        acc = acc_ref[...]
        lane = lax.broadcasted_iota(I32, (tq, LANES), 1)
        o_ref[...] = jnp.where(lane < D_HEAD, acc[:tq], acc[tq:])


def _sb_attn(z, bsz, t_len):
    tq = tkb = min(512, t_len)
    sub = min(128, tkb)
    nq, nk = t_len // tq, t_len // tkb
    qi_tab, kj_tab = _causal_pairs(nq, tq, tkb, True)
    qc, kc, vc = QC0 // LANES, KC0 // LANES, VC0 // LANES
    kern = functools.partial(_sb_attn_kernel, tq=tq, tkb=tkb, sub=sub)
    grid_spec = pltpu.PrefetchScalarGridSpec(
        num_scalar_prefetch=2,
        grid=(bsz, H_C // 2, int(qi_tab.shape[0])),
        in_specs=[pl.BlockSpec((tq, LANES), lambda b, h, p, qt, kt: (b * nq + qt[p], qc + h)),
                  pl.BlockSpec((tkb, LANES), lambda b, h, p, qt, kt: (b * nk + kt[p], kc + h)),
                  pl.BlockSpec((tkb, LANES), lambda b, h, p, qt, kt: (b * nk + kt[p], vc + h))],
        out_specs=pl.BlockSpec((tq, LANES), lambda b, h, p, qt, kt: (b * nq + qt[p], h)),
        scratch_shapes=[pltpu.VMEM((2 * tq, LANES), BF16), pltpu.VMEM((2 * tq, 1), F32),
                        pltpu.VMEM((2 * tq, LANES), F32), pltpu.SMEM((1,), I32)])
    return pl.pallas_call(
        kern, grid_spec=grid_spec,
        out_shape=jax.ShapeDtypeStruct((bsz * t_len, C_C), F32),
        compiler_params=_cp(("parallel", "parallel", "arbitrary")),
        name="sb_attn",
    )(qi_tab, kj_tab, z, z, z)


DEC_ROWS = 16


def _diff_dec_kernel(pt_ref, q_ref, ks_ref, vs_ref, lam_ref, g_ref, *rest, G, lam_init):
    k_refs, v_refs = rest[:G], rest[G:2 * G]
    o_ref, qm_ref, m_ref, l_ref, acc_ref = rest[2 * G:]
    j = pl.program_id(1)
    hd = 2 * D_HEAD
    rowi = lax.broadcasted_iota(I32, (DEC_ROWS, hd), 0)
    lane = lax.broadcasted_iota(I32, (DEC_ROWS, hd), 1)
    sel = (lane // D_HEAD) == rowi

    def head_q(h):
        return jnp.where(sel, q_ref[0][:, h * hd:(h + 1) * hd], 0.0)

    @pl.when(j == 0)
    def _():
        for h in range(H_B):
            qm_ref[h * DEC_ROWS:(h + 1) * DEC_ROWS, :] = head_q(h).astype(BF16)
        m_ref[...] = jnp.full(m_ref.shape, NEG_BIG, F32)
        l_ref[...] = jnp.zeros(l_ref.shape, F32)
        acc_ref[...] = jnp.zeros(acc_ref.shape, F32)

    qm = qm_ref[...]
    head_page = lambda ref, h: ref[pl.ds(h, PAGE_SIZE, stride=H_B), :].astype(BF16)
    s = jnp.concatenate(
        [jnp.concatenate([_dot(qm[h * DEC_ROWS:(h + 1) * DEC_ROWS], head_page(k_refs[gi], h), NT)
                          for gi in range(G)], axis=1) for h in range(H_B)], axis=0)
    m_prev = m_ref[...]
    m_new = jnp.maximum(m_prev, jnp.max(s, axis=-1, keepdims=True))
    alpha = jnp.exp(m_prev - m_new)
    pr = jnp.exp(s - m_new)
    l_ref[...] = alpha * l_ref[...] + jnp.sum(pr, axis=-1, keepdims=True)
    pr = pr.astype(BF16)
    pvs = []
    for h in range(H_B):
        pv = jnp.zeros((DEC_ROWS, hd), F32)
        for gi in range(G):
            pv = pv + _dot(pr[h * DEC_ROWS:(h + 1) * DEC_ROWS, gi * PAGE_SIZE:(gi + 1) * PAGE_SIZE],
                           head_page(v_refs[gi], h))
        pvs.append(pv)
    acc_ref[...] = alpha * acc_ref[...] + jnp.concatenate(pvs, axis=0)
    m_ref[...] = m_new

    @pl.when(j == pl.num_programs(1) - 1)
    def _():
        lam = _lam_value(lam_ref, lam_init)
        outs = []
        for h in range(H_B):
            rs = slice(h * DEC_ROWS, (h + 1) * DEC_ROWS)
            cs = slice(h * hd, (h + 1) * hd)
            s = jnp.sum(head_q(h) * ks_ref[0][:, cs], axis=-1, keepdims=True)
            m_prev = m_ref[rs, :]
            m_new = jnp.maximum(m_prev, s)
            alpha = jnp.exp(m_prev - m_new)
            pr = jnp.exp(s - m_new)
            l_fin = alpha * l_ref[rs, :] + pr
            att = (alpha * acc_ref[rs, :] + pr * vs_ref[0][:, cs]) / l_fin
            o = att[0:1, :] - lam * att[1:2, :]
            ms = jnp.mean(o * o, axis=-1, keepdims=True)
            outs.append(o * lax.rsqrt(ms + RMS_EPS))
        o_ref[0] = jnp.concatenate(outs, axis=-1) * g_ref[...] * (1.0 - lam_init)


def _pages_per_step(n_pages, g):
    while n_pages % g:
        g //= 2
    return g


def _diff_dec(qn, kn, vb, cache_k, cache_v, page_table, layer, lam_w, subln_g, lam_init):
    bsz, n_pages = page_table.shape
    G = _pages_per_step(n_pages, 32)
    nl = cache_k.shape[1]
    hd = 2 * D_HEAD
    ck = cache_k.reshape(cache_k.shape[0], nl, PAGE_SIZE * H_B, hd)
    cv = cache_v.reshape(cache_v.shape[0], nl, PAGE_SIZE * H_B, hd)
    row3 = lambda a: a.reshape(bsz, 1, C_B)
    vec_spec = pl.BlockSpec((1, 1, C_B), lambda b, j, pt: (b, 0, 0))
    page_spec = lambda gi: pl.BlockSpec((None, None, PAGE_SIZE * H_B, hd),
                                        lambda b, j, pt: (pt[b, j * G + gi], layer, 0, 0))
    kern = functools.partial(_diff_dec_kernel, G=G, lam_init=lam_init)
    grid_spec = pltpu.PrefetchScalarGridSpec(
        num_scalar_prefetch=1,
        grid=(bsz, n_pages // G),
        in_specs=[vec_spec, vec_spec, vec_spec,
                  pl.BlockSpec((4, D_HEAD), lambda b, j, pt: (0, 0)),
                  pl.BlockSpec((1, C_B), lambda b, j, pt: (0, 0))]
                 + [page_spec(gi) for gi in range(G)] + [page_spec(gi) for gi in range(G)],
        out_specs=pl.BlockSpec((1, 1, C_B), lambda b, j, pt: (b, 0, 0)),
        scratch_shapes=[pltpu.VMEM((H_B * DEC_ROWS, hd), BF16), pltpu.VMEM((H_B * DEC_ROWS, 1), F32),
                        pltpu.VMEM((H_B * DEC_ROWS, 1), F32), pltpu.VMEM((H_B * DEC_ROWS, hd), F32)])
    out = pl.pallas_call(
        kern, grid_spec=grid_spec,
        out_shape=jax.ShapeDtypeStruct((bsz, 1, C_B), F32),
        compiler_params=_cp(("parallel", "arbitrary")),
        name="diff_dec",
    )(page_table, row3(qn), row3(kn), row3(vb), lam_w,
      jnp.tile(subln_g.reshape(1, 2 * D_HEAD), (1, H_B)), *([ck] * G), *([cv] * G))
    return out.reshape(bsz, C_B)


def _sb_dec_kernel(pt_ref, dn_ref, q_ref, acc0_ref, c0_ref, *rest, G):
    k_refs, v_refs = rest[:G], rest[G:2 * G]
    o_ref, acc_out_ref, c_out_ref, qm_ref, c_ref, acc_ref, done_ref = rest[2 * G:]
    b = pl.program_id(0)
    j = pl.program_id(1)
    shape = (DEC_ROWS, C_C)
    rowi = lax.broadcasted_iota(I32, shape, 0)
    lane = lax.broadcasted_iota(I32, shape, 1)
    sel = (lane // D_HEAD) == rowi

    @pl.when(j == 0)
    def _():
        qm_ref[...] = jnp.where(sel, q_ref[0] * (D_HEAD ** -0.5), 0.0).astype(BF16)
        c_ref[...] = c0_ref[0][:, 0:1]
        acc_ref[...] = acc0_ref[0]
        done_ref[0] = dn_ref[b]

    @pl.when(done_ref[0] == 0)
    def _():
        rr = lax.broadcasted_iota(I32, (PAGE_SIZE, PAGE_SIZE), 0)
        cc = lax.broadcasted_iota(I32, (PAGE_SIZE, PAGE_SIZE), 1)
        upper = jnp.where(rr > cc, 1.0, 0.0).astype(BF16)
        qm = qm_ref[...]
        c_run = c_ref[...]
        acc = acc_ref[...]
        for gi in range(G):
            z = _dot(qm, k_refs[gi][...].astype(BF16))
            logb, l1 = _log_sigmoid_pair(z)
            suf = _mm_exact_rhs(l1, upper, 2)
            att = jnp.exp(logb + suf + c_run)
            acc = acc + _dot(att.astype(BF16), v_refs[gi][...].astype(BF16), NT)
            c_run = c_run + jnp.sum(l1, axis=-1, keepdims=True)
        acc_ref[...] = acc
        c_ref[...] = c_run
        row1 = lax.broadcasted_iota(I32, (DEC_ROWS, 1), 0)
        cmax = jnp.max(jnp.where(row1 < H_C, c_run, NEG_BIG))
        done_ref[0] = (cmax < SB_DEAD).astype(I32)

    @pl.when(j == pl.num_programs(1) - 1)
    def _():
        acc = acc_ref[...]
        o_ref[0] = jnp.sum(jnp.where(sel, acc, 0.0), axis=0, keepdims=True)
        acc_out_ref[0] = acc
        c_out_ref[0] = jnp.broadcast_to(c_ref[...], (DEC_ROWS, LANES))


def _sb_dec_phase(q3, ck, cv, page_table, done, acc0, c0, layer, start, n_phase, G):
    bsz, n_pages = page_table.shape

    def page_spec(gi):
        def idx(b, j, pt, dn):
            page = pt[b, n_pages - 1 - (start + j * G + gi)]
            return (jnp.where(dn[b] == 0, page, pt[0, 0]), layer, 0, 0)
        return pl.BlockSpec((None, None, C_C, PAGE_SIZE), idx)

    row = lambda shape: pl.BlockSpec((1,) + shape, lambda b, j, pt, dn: (b, 0, 0))
    kern = functools.partial(_sb_dec_kernel, G=G)
    grid_spec = pltpu.PrefetchScalarGridSpec(
        num_scalar_prefetch=2,
        grid=(bsz, n_phase // G),
        in_specs=[row((1, C_C)), row((DEC_ROWS, C_C)), row((DEC_ROWS, LANES))]
                 + [page_spec(gi) for gi in range(G)] + [page_spec(gi) for gi in range(G)],
        out_specs=[row((1, C_C)), row((DEC_ROWS, C_C)), row((DEC_ROWS, LANES))],
        scratch_shapes=[pltpu.VMEM((DEC_ROWS, C_C), BF16), pltpu.VMEM((DEC_ROWS, 1), F32),
                        pltpu.VMEM((DEC_ROWS, C_C), F32), pltpu.SMEM((1,), I32)])
    return pl.pallas_call(
        kern, grid_spec=grid_spec,
        out_shape=[jax.ShapeDtypeStruct((bsz, 1, C_C), F32), jax.ShapeDtypeStruct((bsz, DEC_ROWS, C_C), F32),
                   jax.ShapeDtypeStruct((bsz, DEC_ROWS, LANES), F32)],
        compiler_params=_cp(("parallel", "arbitrary")),
        name="sb_dec",
    )(page_table, done, q3, acc0, c0, *([ck] * G), *([cv] * G))


SB_DEC_HEAD_PAGES = 8
SB_DEC_TAIL_GROUP = 12


def _sb_dec(qc, cache_k, cache_v, page_table, layer):
    bsz, n_pages = page_table.shape
    nl = cache_k.shape[1]
    ck = jnp.transpose(cache_k, (0, 1, 3, 4, 2)).reshape(cache_k.shape[0], nl, C_C, PAGE_SIZE)
    cv = jnp.transpose(cache_v, (0, 1, 3, 4, 2)).reshape(cache_v.shape[0], nl, C_C, PAGE_SIZE)
    q3 = qc.reshape(bsz, 1, C_C)
    n_head = min(SB_DEC_HEAD_PAGES, n_pages)
    acc0 = jnp.zeros((bsz, DEC_ROWS, C_C), F32)
    c0 = jnp.zeros((bsz, DEC_ROWS, LANES), F32)
    live = jnp.zeros((bsz,), I32)
    out, acc, c = _sb_dec_phase(q3, ck, cv, page_table, live, acc0, c0, layer, 0, n_head, n_head)
    n_tail = n_pages - n_head
    if n_tail:
        done = (jnp.max(c[:, :H_C, 0], axis=1) < SB_DEAD).astype(I32)
        tail = lambda: _sb_dec_phase(q3, ck, cv, page_table, done, acc, c, layer, n_head, n_tail,
                                     _pages_per_step(n_tail, SB_DEC_TAIL_GROUP))[0]
        out = lax.cond(jnp.all(done == 1), lambda: out, tail)
    return out.reshape(bsz, C_C)


def _merge_kernel(h_ref, oa_ref, ob_ref, oc_ref, g0_ref, g1_ref, g2_ref, wa_ref, wb_ref, wc_ref, wo_ref,
                  gn_ref, rw_ref, rb_ref, h_out_ref, xn_ref, ridx_ref, rgate_ref):
    mix = (_sigmoid(g0_ref[...]) * _dot(oa_ref[...].astype(BF16), wa_ref[...])
           + _sigmoid(g1_ref[...]) * _dot(ob_ref[...].astype(BF16), wb_ref[...])
           + _sigmoid(g2_ref[...]) * _dot(oc_ref[...].astype(BF16), wc_ref[...]))
    h = h_ref[...] + _dot(mix.astype(BF16), wo_ref[...])
    h_out_ref[...] = h
    ms = jnp.mean(h * h, axis=-1, keepdims=True)
    xn = h * lax.rsqrt(ms + RMS_EPS) * gn_ref[...]
    for s in range(ROW_TILES):
        xn_ref[pl.ds(s, xn.shape[0], stride=ROW_TILES), :] = xn[:, s * LANES:(s + 1) * LANES]
    logits = _mm(xn, rw_ref[...], NN, 3) + rb_ref[...]
    tm = logits.shape[0]
    lane = lax.broadcasted_iota(I32, logits.shape, 1)
    lane_o = lax.broadcasted_iota(I32, (tm, LANES), 1)
    work = logits
    vals, idxs = [], []
    for _ in range(TOP_K):
        mx = jnp.max(work, axis=-1, keepdims=True)
        ik = jnp.min(jnp.where(work == mx, lane, N_EXPERTS), axis=-1, keepdims=True)
        vals.append(mx)
        idxs.append(ik)
        work = jnp.where(lane == ik, -jnp.inf, work)
    es = [jnp.exp(vk - vals[0]) for vk in vals]
    den = es[0] + es[1] + es[2] + es[3]
    ridx = jnp.zeros((tm, LANES), I32)
    rgate = jnp.zeros((tm, LANES), F32)
    for kx in range(TOP_K):
        ridx = jnp.where(lane_o == kx, idxs[kx], ridx)
        rgate = jnp.where(lane_o == kx, es[kx] / den, rgate)
    ridx_ref[...] = ridx
    rgate_ref[...] = rgate


def _merge(h, o_a, o_b, o_c, z, W):
    n, d = h.shape
    tm = min(512, n)
    gcol = ZG0 // d
    row = lambda c: pl.BlockSpec((tm, c), lambda i: (i, 0))
    full = lambda a: pl.BlockSpec(a.shape, lambda i: (0,) * a.ndim)
    wa, wb, wc, wo = (W['w_branch_a'].astype(BF16), W['w_branch_b'].astype(BF16),
                      W['w_branch_c'].astype(BF16), W['w_out'].astype(BF16))
    gn = W['norm_ffn_g'].reshape(1, d)
    rw = W['router_w']
    rb = W['router_b'].reshape(1, N_EXPERTS)
    return pl.pallas_call(
        _merge_kernel,
        grid=(n // tm,),
        in_specs=[row(d), row(C_A), row(C_B), row(C_C),
                  pl.BlockSpec((tm, d), lambda i: (i, gcol)),
                  pl.BlockSpec((tm, d), lambda i: (i, gcol + 1)),
                  pl.BlockSpec((tm, d), lambda i: (i, gcol + 2)),
                  full(wa), full(wb), full(wc), full(wo), full(gn), full(rw), full(rb)],
        out_specs=[row(d), pl.BlockSpec((tm * ROW_TILES, LANES), lambda i: (i, 0)), row(LANES), row(LANES)],
        out_shape=[jax.ShapeDtypeStruct((n, d), F32), jax.ShapeDtypeStruct((n * ROW_TILES, LANES), F32),
                   jax.ShapeDtypeStruct((n, LANES), I32), jax.ShapeDtypeStruct((n, LANES), F32)],
        compiler_params=_cp(("parallel",), VMEM_LIMIT),
        name="merge",
    )(h, o_a, o_b, o_c, z, z, z, wa, wb, wc, wo, gn, rw, rb)


def _route_kernel(ridx_ref, rpos_ref, cnt_ref, base_ref):
    i = pl.program_id(0)

    @pl.when(i == 0)
    def _():
        base_ref[...] = jnp.zeros(base_ref.shape, F32)

    ridx = ridx_ref[...]
    tm = ridx.shape[0]
    lane_e = lax.broadcasted_iota(I32, (tm, N_EXPERTS), 1)
    lane_o = lax.broadcasted_iota(I32, (tm, LANES), 1)
    onehots = [(lane_e == ridx[:, kx:kx + 1]).astype(F32) for kx in range(TOP_K)]
    sel = onehots[0] + onehots[1] + onehots[2] + onehots[3]
    rr = lax.broadcasted_iota(I32, (tm, tm), 0)
    cc = lax.broadcasted_iota(I32, (tm, tm), 1)
    before = jnp.where(cc < rr, 1.0, 0.0).astype(BF16)
    rank = _dot(before, sel.astype(BF16)) + base_ref[...]
    rpos = jnp.zeros((tm, LANES), I32)
    for kx in range(TOP_K):
        pk = jnp.sum(onehots[kx] * rank, axis=-1, keepdims=True)
        rpos = jnp.where(lane_o == kx, pk.astype(I32), rpos)
    rpos_ref[...] = rpos
    base_ref[...] = base_ref[...] + jnp.sum(sel, axis=0, keepdims=True)
    cnt_ref[...] = base_ref[...]


def _route(ridx):
    n = ridx.shape[0]
    tm = min(256, n)
    return pl.pallas_call(
        _route_kernel,
        grid=(n // tm,),
        in_specs=[pl.BlockSpec((tm, LANES), lambda i: (i, 0))],
        out_specs=[pl.BlockSpec((tm, LANES), lambda i: (i, 0)), pl.BlockSpec((1, N_EXPERTS), lambda i: (0, 0))],
        out_shape=[jax.ShapeDtypeStruct((n, LANES), I32), jax.ShapeDtypeStruct((1, N_EXPERTS), F32)],
        scratch_shapes=[pltpu.VMEM((1, N_EXPERTS), F32)],
        compiler_params=_cp(("arbitrary",)),
        name="route",
    )(ridx)


def _row_copy(src, src_row, dst, dst_row, sem):
    return pltpu.make_async_copy(src.at[pl.ds(pl.multiple_of(src_row * ROW_TILES, ROW_TILES), ROW_TILES)],
                                 dst.at[pl.ds(pl.multiple_of(dst_row * ROW_TILES, ROW_TILES), ROW_TILES)], sem)


def _dispatch_kernel(dest_ref, tail_ref, nused_ref, x_ref, xs_ref, zbuf_ref, sem_z, sem, *, tm, blk, n_blk):
    i = pl.program_id(0)

    def zero_copy(row0):
        return pltpu.make_async_copy(
            zbuf_ref, xs_ref.at[pl.ds(pl.multiple_of(row0 * ROW_TILES, ROW_TILES), blk * ROW_TILES)], sem_z)

    @pl.when(i == 0)
    def _():
        zbuf_ref[...] = jnp.zeros(zbuf_ref.shape, F32)
        for e in range(N_EXPERTS):
            pl.when(tail_ref[e] >= 0)(lambda e=e: zero_copy(tail_ref[e]).start())
        for e in range(N_EXPERTS):
            pl.when(tail_ref[e] >= 0)(lambda e=e: zero_copy(tail_ref[e]).wait())

        def fill(b, carry):
            cp = zero_copy(b * blk)
            cp.start()
            cp.wait()
            return carry

        lax.fori_loop(nused_ref[0], n_blk, fill, 0)

    def issue(r, carry):
        for kx in range(TOP_K):
            _row_copy(x_ref, r, xs_ref, dest_ref[r * TOP_K + kx], sem).start(priority=kx % 2)
        return carry

    lax.fori_loop(0, tm, issue, 0)

    def drain(r, carry):
        for kx in range(TOP_K):
            _row_copy(x_ref, 0, xs_ref, 0, sem).wait()
        return carry

    lax.fori_loop(0, tm, drain, 0)


def _dispatch(xn_tiles, dest_flat, tail_rows, nused, n_blk, blk):
    n = xn_tiles.shape[0] // ROW_TILES
    tm = min(256, n)
    kern = functools.partial(_dispatch_kernel, tm=tm, blk=blk, n_blk=n_blk)
    return pl.pallas_call(
        kern,
        grid=(n // tm,),
        in_specs=[pl.BlockSpec((tm * TOP_K,), lambda i: (i,), memory_space=pltpu.SMEM),
                  pl.BlockSpec(memory_space=pltpu.SMEM),
                  pl.BlockSpec(memory_space=pltpu.SMEM),
                  pl.BlockSpec((tm * ROW_TILES, LANES), lambda i: (i, 0))],
        out_specs=pl.BlockSpec(memory_space=pl.ANY),
        out_shape=jax.ShapeDtypeStruct((n_blk * blk * ROW_TILES, LANES), F32),
        scratch_shapes=[pltpu.VMEM((blk * ROW_TILES, LANES), F32),
                        pltpu.SemaphoreType.DMA(()), pltpu.SemaphoreType.DMA(())],
        compiler_params=_cp(("arbitrary",)),
        name="moe_dispatch",
    )(dest_flat, tail_rows, nused, xn_tiles)


def _rows_from_tiles(ref, n):
    return jnp.concatenate([ref[pl.ds(s, n, stride=ROW_TILES), :] for s in range(ROW_TILES)], axis=1)


def _expert_kernel(blk_e, nused, x_ref, wgu_ref, bgu_ref, wd_ref, bd_ref, y_ref, wgu_bf, wd_bf, *, blk, d_ff):
    i = pl.program_id(0)

    @pl.when(i < nused[0])
    def _():
        e = blk_e[i]
        e_prev = blk_e[jnp.maximum(i - 1, 0)]

        @pl.when(jnp.logical_or(i == 0, e != e_prev))
        def _():
            wgu_bf[...] = wgu_ref[...].astype(BF16)
            wd_bf[...] = wd_ref[...].astype(BF16)

        x = _rows_from_tiles(x_ref, blk).astype(BF16)
        hh = _dot(x, wgu_bf[...]) + bgu_ref[...]
        hg = jnp.minimum(hh[:, :d_ff], SWIGLU_LIMIT)
        hl = jnp.clip(hh[:, d_ff:], -SWIGLU_LIMIT, SWIGLU_LIMIT)
        act = hg * _sigmoid(SWIGLU_ALPHA * hg) * (hl + 1.0)
        y = _dot(act.astype(BF16), wd_bf[...]) + bd_ref[...]
        for s in range(ROW_TILES):
            y_ref[pl.ds(s, blk, stride=ROW_TILES), :] = y[:, s * LANES:(s + 1) * LANES]

    @pl.when(i >= nused[0])
    def _():
        y_ref[...] = jnp.zeros(y_ref.shape, F32)


def _experts(xs, blk_e, nused, w_gu, b_gu, w_down, b_down, layer, blk):
    n_blk = blk_e.shape[0]
    _, n_e, d, f2 = w_gu.shape
    d_ff = f2 // 2
    kern = functools.partial(_expert_kernel, blk=blk, d_ff=d_ff)
    last = lambda i, be, nu: jnp.minimum(i, nu[0] - 1)
    grid_spec = pltpu.PrefetchScalarGridSpec(
        num_scalar_prefetch=2,
        grid=(n_blk,),
        in_specs=[pl.BlockSpec((blk * ROW_TILES, LANES), lambda i, be, nu: (last(i, be, nu), 0)),
                  pl.BlockSpec((None, None, d, f2), lambda i, be, nu: (layer, be[i], 0, 0)),
                  pl.BlockSpec((None, 1, f2), lambda i, be, nu: (be[i], 0, 0)),
                  pl.BlockSpec((None, None, d_ff, d), lambda i, be, nu: (layer, be[i], 0, 0)),
                  pl.BlockSpec((None, 1, d), lambda i, be, nu: (be[i], 0, 0))],
        out_specs=pl.BlockSpec((blk * ROW_TILES, LANES), lambda i, be, nu: (i, 0)),
        scratch_shapes=[pltpu.VMEM((d, f2), BF16), pltpu.VMEM((d_ff, d), BF16)])
    return pl.pallas_call(
        kern, grid_spec=grid_spec,
        out_shape=jax.ShapeDtypeStruct(xs.shape, F32),
        compiler_params=_cp(("arbitrary",), VMEM_LIMIT),
        name="moe_experts",
    )(blk_e, nused, xs, w_gu, b_gu.reshape(n_e, 1, f2), w_down, b_down.reshape(n_e, 1, d))


def _combine_kernel(dest_ref, dest_next_ref, gate_ref, h_ref, pe_ref, yb_ref, gp_ref, pw_ref, pg_ref, o_ref,
                    buf_ref, sems, *, tm, nsteps):
    i = pl.program_id(0)
    slot = i % 2

    def gather(dref, sl):
        def issue(r, carry):
            for kx in range(TOP_K):
                _row_copy(yb_ref, dref[r * TOP_K + kx], buf_ref.at[sl, kx], r, sems.at[sl]).start(priority=kx % 2)
            return carry

        lax.fori_loop(0, tm, issue, 0)

    pl.when(i == 0)(lambda: gather(dest_ref, 0))
    pl.when(i + 1 < nsteps)(lambda: gather(dest_next_ref, 1 - slot))

    def drain(r, carry):
        for kx in range(TOP_K):
            _row_copy(yb_ref, 0, buf_ref.at[slot, kx], 0, sems.at[slot]).wait()
        return carry

    lax.fori_loop(0, tm, drain, 0)

    gate = gate_ref[...]
    h = h_ref[...]
    for kx in range(TOP_K):
        h = h + gate[:, kx:kx + 1] * _rows_from_tiles(buf_ref.at[slot, kx], tm)
    ms = jnp.mean(h * h, axis=-1, keepdims=True)
    xn = (h * lax.rsqrt(ms + RMS_EPS) * gp_ref[...]).astype(BF16)
    o_ref[...] = h + _dot(pe_ref[...].astype(BF16), pw_ref[...]) * _sigmoid(_dot(xn, pg_ref[...]))


def _combine(h, pe, yb, dest_flat, rgate, W):
    n, d = h.shape
    tm = min(256, n)
    gp = W['norm_ple_g'].reshape(1, d)
    pw = W['ple_w'].astype(BF16)
    pg = W['ple_gate_w'].astype(BF16)
    full = lambda a: pl.BlockSpec(a.shape, lambda i: (0,) * a.ndim)
    nsteps = n // tm
    kern = functools.partial(_combine_kernel, tm=tm, nsteps=nsteps)
    return pl.pallas_call(
        kern,
        grid=(nsteps,),
        in_specs=[pl.BlockSpec((tm * TOP_K,), lambda i: (i,), memory_space=pltpu.SMEM),
                  pl.BlockSpec((tm * TOP_K,), lambda i: (jnp.minimum(i + 1, nsteps - 1),), memory_space=pltpu.SMEM),
                  pl.BlockSpec((tm, LANES), lambda i: (i, 0)),
                  pl.BlockSpec((tm, d), lambda i: (i, 0)),
                  pl.BlockSpec((tm, pe.shape[1]), lambda i: (i, 0)),
                  pl.BlockSpec(memory_space=pl.ANY),
                  full(gp), full(pw), full(pg)],
        out_specs=pl.BlockSpec((tm, d), lambda i: (i, 0)),
        out_shape=jax.ShapeDtypeStruct((n, d), F32),
        scratch_shapes=[pltpu.VMEM((2, TOP_K, tm * ROW_TILES, LANES), F32), pltpu.SemaphoreType.DMA((2,))],
        compiler_params=_cp(("arbitrary",)),
        name="moe_combine",
    )(dest_flat, dest_flat, rgate, h, pe, yb, gp, pw, pg)


def _moe_and_ple(h, xn, ridx, rgate, pe, W):
    n, d = h.shape
    m = n * TOP_K
    blk = max(16, min(512, (m // N_EXPERTS) // 16 * 16))
    n_blk = -(-m // blk) + N_EXPERTS
    rpos, counts = _route(ridx)
    counts = counts[0].astype(I32)
    padded = (counts + blk - 1) // blk * blk
    pad_end = jnp.cumsum(padded)
    pad_start = pad_end - padded
    dest = pad_start[ridx[:, :TOP_K]] + rpos[:, :TOP_K]
    dest_flat = dest.reshape(m).astype(I32)
    blk_row0 = jnp.arange(n_blk, dtype=I32) * blk
    blk_e = jnp.minimum(jnp.sum((pad_end[None, :] <= blk_row0[:, None]).astype(I32), axis=1), N_EXPERTS - 1)
    nused = (pad_end[-1] // blk).astype(I32).reshape(1)
    blk_e = jnp.where(jnp.arange(n_blk) < nused[0], blk_e, blk_e[jnp.maximum(nused[0] - 1, 0)])
    tail_rows = jnp.where(counts % blk != 0, pad_end - blk, -1).astype(I32)
    xn_tiles = xn
    xs = _dispatch(xn_tiles, dest_flat, tail_rows, nused, n_blk, blk)
    yb = _experts(xs, blk_e, nused, W['moe_w_gu'], W['moe_b_gu'], W['moe_w_down'], W['moe_b_down'], W['layer'], blk)
    return _combine(h, pe, yb, dest_flat, rgate, W)


def _layer_common(h, z, o_a, o_b, o_c, pe, W):
    h_mid, xn, ridx, rgate = _merge(h, o_a, o_b, o_c, z, W)
    return _moe_and_ple(h_mid, xn, ridx, rgate, pe, W)


def _layer_prompt(h, pe, wp, W, bsz, t_len, lam_init):
    n = bsz * t_len
    z = _in_proj(h, W['norm_mix_g'], wp)
    L = min(64, t_len)
    shift0 = jnp.zeros((bsz, C_RWKV), F32)
    wkv0 = jnp.zeros((bsz, H_A, D_HEAD, D_HEAD), F32)
    o_a, wkv_new, shift_new = _rwkv(z, shift0, wkv0, W, bsz, t_len, L)
    qn, kn = _qknorm(z, W['diff_q_norm'], W['diff_k_norm'])
    o_b = _diff_attn(qn, kn, z, W['diff_lambda'], W['diff_subln_g'], bsz, t_len, lam_init)
    o_c = _sb_attn(z, bsz, t_len)
    h = _layer_common(h, z, o_a, o_b, o_c, pe, W)
    rows = (kn.reshape(bsz, t_len, H_B, 2 * D_HEAD),
            z[:, VB0:VB0 + C_B].reshape(bsz, t_len, H_B, 2 * D_HEAD),
            z[:, KC0:KC0 + C_C].reshape(bsz, t_len, H_C, D_HEAD),
            z[:, VC0:VC0 + C_C].reshape(bsz, t_len, H_C, D_HEAD),
            wkv_new, shift_new)
    return h, rows


DEC_CHUNK = 16


def _layer_sample(h, pe, wp, W, caches, page_table, shift0, wkv0, layer, lam_init):
    bsz = h.shape[0]
    z = _in_proj(h, W['norm_mix_g'], wp)
    za = jnp.pad(z[:, None, :ZA_W], ((0, 0), (0, DEC_CHUNK - 1), (0, 0))).reshape(bsz * DEC_CHUNK, ZA_W)
    o_a, wkv_new, shift_new = _rwkv(za, shift0, wkv0, W, bsz, 1, DEC_CHUNK)
    o_a = o_a.reshape(bsz, DEC_CHUNK, C_A)[:, 0]
    qn, kn = _qknorm(z, W['diff_q_norm'], W['diff_k_norm'])
    vb = z[:, VB0:VB0 + C_B]
    o_b = _diff_dec(qn, kn, vb, caches[0], caches[1], page_table, layer, W['diff_lambda'], W['diff_subln_g'], lam_init)
    o_c = _sb_dec(z[:, QC0:QC0 + C_C], caches[2], caches[3], page_table, layer)
    h = _layer_common(h, z, o_a, o_b, o_c, pe, W)
    rows = (kn.reshape(bsz, 1, H_B, 2 * D_HEAD), vb.reshape(bsz, 1, H_B, 2 * D_HEAD),
            z[:, KC0:KC0 + C_C].reshape(bsz, 1, H_C, D_HEAD), z[:, VC0:VC0 + C_C].reshape(bsz, 1, H_C, D_HEAD),
            wkv_new, shift_new)
    return h, rows


def kernel(x_prompt, x_sample, cache_diff_k, cache_diff_v, cache_sb_k, cache_sb_v, state_wkv, state_shift, page_table, p_prompt, p_sample, norm_mix_g, w_in, rwkv_mu, rwkv_w0, rwkv_w2, rwkv_a0, rwkv_a2, rwkv_g2, rwkv_k_k, rwkv_k_a, rwkv_r_k, rwkv_lnx_g, rwkv_lnx_b, diff_q_norm, diff_k_norm, diff_lambda, diff_subln_g, w_branch_a, w_branch_b, w_branch_c, w_out, norm_ffn_g, router_w, router_b, moe_w_gu, moe_b_gu, moe_w_down, moe_b_down, norm_ple_g, ple_w, ple_gate_w):
    bsz_p, seq_p, d = x_prompt.shape
    bsz_s, seq_s, _ = x_sample.shape
    assert seq_s == 1 and d == ROW_TILES * LANES
    depth = w_in.shape[0]
    params = dict(norm_mix_g=norm_mix_g, rwkv_mu=rwkv_mu, rwkv_w0=rwkv_w0, rwkv_w2=rwkv_w2, rwkv_a0=rwkv_a0,
                  rwkv_a2=rwkv_a2, rwkv_g2=rwkv_g2, rwkv_k_k=rwkv_k_k, rwkv_k_a=rwkv_k_a, rwkv_r_k=rwkv_r_k,
                  rwkv_lnx_g=rwkv_lnx_g, rwkv_lnx_b=rwkv_lnx_b, diff_q_norm=diff_q_norm, diff_k_norm=diff_k_norm,
                  diff_lambda=diff_lambda, diff_subln_g=diff_subln_g, w_branch_a=w_branch_a,
                  w_branch_b=w_branch_b, w_branch_c=w_branch_c, w_out=w_out, norm_ffn_g=norm_ffn_g,
                  router_w=router_w, router_b=router_b, moe_w_gu=moe_w_gu, moe_b_gu=moe_b_gu,
                  moe_w_down=moe_w_down, moe_b_down=moe_b_down, norm_ple_g=norm_ple_g, ple_w=ple_w,
                  ple_gate_w=ple_gate_w)
    caches = (cache_diff_k, cache_diff_v, cache_sb_k, cache_sb_v)
    h_p = x_prompt.reshape(bsz_p * seq_p, d)
    h_s = x_sample.reshape(bsz_s, d)
    rows_p, rows_s = [], []
    for i in range(depth):
        W = {name: (val if name in ('moe_w_gu', 'moe_w_down') else val[i]) for name, val in params.items()}
        W['layer'] = i
        wp = (w_in, i)
        lam_init = 0.8 - 0.6 * math.exp(-0.3 * i)
        h_p, r_p = _layer_prompt(h_p, p_prompt[i].reshape(bsz_p * seq_p, -1), wp, W, bsz_p, seq_p, lam_init)
        h_s, r_s = _layer_sample(h_s, p_sample[i].reshape(bsz_s, -1), wp, W, caches, page_table,
                                 state_shift[:, i], state_wkv[:, i], i, lam_init)
        rows_p.append(r_p)
        rows_s.append(r_s)
    st = lambda rows, j: jnp.stack([r[j] for r in rows], axis=1)
    return (h_p.reshape(bsz_p, seq_p, d), h_s.reshape(bsz_s, seq_s, d),
            st(rows_p, 0), st(rows_p, 1), st(rows_p, 2), st(rows_p, 3), st(rows_p, 4), st(rows_p, 5),
            st(rows_s, 0), st(rows_s, 1), st(rows_s, 2), st(rows_s, 3), st(rows_s, 4), st(rows_s, 5))
```
